```python
import math
import jax, jax.numpy as jnp
from jax import lax
import numpy as np

D_MODEL = 1024
BATCH = 8
SEQ = 2048
DEPTH = 4

D_FF = 2816
FFN_HALF = 0.5
DN_HEAD_DIM = 128
DN_HEADS = D_MODEL // 128
DN_WIDTH = DN_HEADS * DN_HEAD_DIM
DN_CONV = 4
DN_CHUNK = 64
SB_HEAD_DIM = 128
SB_HEADS = D_MODEL // 128
SB_WIDTH = SB_HEADS * SB_HEAD_DIM
SB_BLOCK = 128
IN_SIZES = (3 * DN_WIDTH, DN_WIDTH, DN_HEADS, DN_HEADS, SB_WIDTH, SB_WIDTH, SB_WIDTH, D_MODEL, D_MODEL)
N_IN = 4 * DN_WIDTH + 2 * DN_HEADS + 3 * SB_WIDTH + 2 * D_MODEL
RMS_EPS = 1e-6
L2_EPS = 1e-6

kernel_name = "hybrid_deltanet_stickbreaking_macaron"


def rmsnorm(x, gain):
    xf = x.astype(jnp.float32)
    y = xf * lax.rsqrt(jnp.mean(xf * xf, axis=-1, keepdims=True) + RMS_EPS)
    return (y * gain.astype(jnp.float32)).astype(x.dtype)


def l2norm(x):
    xf = x.astype(jnp.float32)
    return xf * lax.rsqrt(jnp.sum(xf * xf, axis=-1, keepdims=True) + L2_EPS)


def swiglu(h, w_in, w_out):
    gate, up = jnp.split(h @ w_in, 2, axis=-1)
    return (jax.nn.silu(gate) * up) @ w_out


def causal_depthwise_conv(x, w):
    width, ch = w.shape
    return lax.conv_general_dilated(
        x, w[:, None, :], window_strides=(1,), padding=((width - 1, 0),),
        dimension_numbers=("NWC", "WIO", "NWC"), feature_group_count=ch)


def split_cols(proj):
    parts, start = [], 0
    for size in IN_SIZES:
        parts.append(proj[..., start:start + size])
        start += size
    return parts


def to_heads(t, n_heads, head_dim):
    b, s, _ = t.shape
    return t.reshape(b, s, n_heads, head_dim).transpose(0, 2, 1, 3)


def gated_delta_rule_chunked(q, k, v, g, beta):
    b, h, t, dk = q.shape
    dv = v.shape[-1]
    c = DN_CHUNK
    n = t // c
    q = (q * (dk ** -0.5)).reshape(b, h, n, c, dk)
    k = k.reshape(b, h, n, c, dk)
    v = v.astype(jnp.float32).reshape(b, h, n, c, dv)
    beta = beta.reshape(b, h, n, c)
    g = lax.cumsum(g.reshape(b, h, n, c), axis=3)
    idx = jnp.arange(c)
    lower_incl = idx[:, None] >= idx[None, :]
    strict = idx[:, None] > idx[None, :]
    decay = jnp.exp(jnp.where(lower_incl, g[..., :, None] - g[..., None, :], -jnp.inf))
    k_beta = k * beta[..., None]
    lmat = jnp.where(strict, jnp.einsum("bhnid,bhnjd->bhnij", k_beta, k) * decay, 0.0)
    amat = lmat + jnp.eye(c, dtype=jnp.float32)
    rhs = jnp.concatenate([v * beta[..., None], k_beta * jnp.exp(g)[..., None]], axis=-1)
    sol = lax.linalg.triangular_solve(amat, rhs, left_side=True, lower=True, unit_diagonal=True)
    u, w = sol[..., :dv], sol[..., dv:]
    attn_intra = jnp.einsum("bhnid,bhnjd->bhnij", q, k) * decay
    q_dec = q * jnp.exp(g)[..., None]
    g_last = g[..., -1]
    k_dec = k * jnp.exp(g_last[..., None] - g)[..., None]

    def chunk_step(state, inp):
        q_i, k_i, u_i, w_i, a_i, gl_i = inp
        v_new = u_i - jnp.einsum("bhcd,bhdv->bhcv", w_i, state)
        o_i = jnp.einsum("bhcd,bhdv->bhcv", q_i, state) + jnp.einsum("bhij,bhjv->bhiv", a_i, v_new)
        state = state * jnp.exp(gl_i)[..., None, None] + jnp.einsum("bhcd,bhcv->bhdv", k_i, v_new)
        return state, o_i

    chunk_first = lambda arr: jnp.moveaxis(arr, 2, 0)
    s0 = jnp.zeros((b, h, dk, dv), jnp.float32)
    _, o = lax.scan(chunk_step, s0, (chunk_first(q_dec), chunk_first(k_dec), chunk_first(u),
                                     chunk_first(w), chunk_first(attn_intra), chunk_first(g_last)))
    return jnp.moveaxis(o, 0, 2).reshape(b, h, t, dv)


def stick_breaking_attention(q, k, v):
    t = q.shape[2]
    scale = q.shape[-1] ** -0.5
    outs = []
    for blk in range(t // SB_BLOCK):
        t0, t1 = blk * SB_BLOCK, (blk + 1) * SB_BLOCK
        z = jnp.einsum("bhqd,bhkd->bhqk", q[:, :, t0:t1], k[:, :, :t1]).astype(jnp.float32) * scale
        causal = jnp.arange(t1)[None, :] < (t0 + jnp.arange(SB_BLOCK))[:, None]
        log_1mb = jnp.where(causal, -jax.nn.softplus(z), 0.0)
        survive = lax.cumsum(log_1mb, axis=3, reverse=True) - log_1mb
        weights = jnp.where(causal, jnp.exp(jax.nn.log_sigmoid(z) + survive), 0.0)
        outs.append(jnp.einsum("bhqk,bhkd->bhqd", weights.astype(v.dtype), v[:, :, :t1]))
    return jnp.concatenate(outs, axis=2)


def hybrid_mixer(h, w_in, conv_w, a_log, dt_bias, dn_out_norm, sb_q_norm, sb_k_norm,
                 w_branch_a, w_branch_b, w_out):
    b, t, _ = h.shape
    dn_qkv, dn_z, dn_b, dn_a, sb_q, sb_k, sb_v, gate_a, gate_b = split_cols(h @ w_in)
    qkv = jax.nn.silu(causal_depthwise_conv(dn_qkv, conv_w))
    q, k, v = jnp.split(qkv, 3, axis=-1)
    q = l2norm(to_heads(q, DN_HEADS, DN_HEAD_DIM))
    k = l2norm(to_heads(k, DN_HEADS, DN_HEAD_DIM))
    v = to_heads(v, DN_HEADS, DN_HEAD_DIM)
    beta = jax.nn.sigmoid(dn_b.astype(jnp.float32)).transpose(0, 2, 1)
    g = (-jnp.exp(a_log.astype(jnp.float32))
         * jax.nn.softplus(dn_a.astype(jnp.float32) + dt_bias.astype(jnp.float32))).transpose(0, 2, 1)
    o_a = gated_delta_rule_chunked(q, k, v, g, beta).transpose(0, 2, 1, 3)
    o_a = rmsnorm(o_a, dn_out_norm) * jax.nn.silu(dn_z.reshape(b, t, DN_HEADS, DN_HEAD_DIM).astype(jnp.float32))
    y_a = o_a.reshape(b, t, DN_WIDTH).astype(h.dtype) @ w_branch_a
    qb = rmsnorm(to_heads(sb_q, SB_HEADS, SB_HEAD_DIM), sb_q_norm)
    kb = rmsnorm(to_heads(sb_k, SB_HEADS, SB_HEAD_DIM), sb_k_norm)
    vb = to_heads(sb_v, SB_HEADS, SB_HEAD_DIM)
    o_b = stick_breaking_attention(qb, kb, vb).transpose(0, 2, 1, 3).reshape(b, t, SB_WIDTH)
    y_b = o_b @ w_branch_b
    merged = jax.nn.sigmoid(gate_a) * y_a + jax.nn.sigmoid(gate_b) * y_b
    return merged @ w_out


def setup_inputs(seed: int = 0) -> dict:
    key = jax.random.key(seed)
    ks = jax.random.split(key, 20)
    L = DEPTH

    def dense(k, shape, fan_in):
        return jax.random.normal(k, shape, jnp.float32) * (fan_in ** -0.5)

    def gain(k, shape):
        return 1.0 + 0.1 * jax.random.normal(k, shape, jnp.float32)

    dt = jnp.exp(jax.random.uniform(ks[8], (L, DN_HEADS), jnp.float32, math.log(1e-3), math.log(1e-1)))
    return {
        "x": jax.random.normal(ks[0], (BATCH, SEQ, D_MODEL), jnp.float32),
        "ffn1_norm": gain(ks[1], (L, D_MODEL)),
        "ffn1_w_in": dense(ks[2], (L, D_MODEL, 2 * D_FF), D_MODEL),
        "ffn1_w_out": dense(ks[3], (L, D_FF, D_MODEL), D_FF),
        "mix_norm": gain(ks[4], (L, D_MODEL)),
        "w_in": dense(ks[5], (L, D_MODEL, N_IN), D_MODEL),
        "dn_conv_w": dense(ks[6], (L, DN_CONV, 3 * DN_WIDTH), DN_CONV),
        "dn_a_log": jnp.log(jax.random.uniform(ks[7], (L, DN_HEADS), jnp.float32, 1.0, 16.0)),
        "dn_dt_bias": dt + jnp.log(-jnp.expm1(-dt)),
        "dn_out_norm": gain(ks[9], (L, DN_HEAD_DIM)),
        "sb_q_norm": gain(ks[10], (L, SB_HEAD_DIM)),
        "sb_k_norm": gain(ks[11], (L, SB_HEAD_DIM)),
        "w_branch_a": dense(ks[12], (L, DN_WIDTH, D_MODEL), DN_WIDTH),
        "w_branch_b": dense(ks[13], (L, SB_WIDTH, D_MODEL), SB_WIDTH),
        "w_out": dense(ks[14], (L, D_MODEL, D_MODEL), D_MODEL),
        "ffn2_norm": gain(ks[15], (L, D_MODEL)),
        "ffn2_w_in": dense(ks[16], (L, D_MODEL, 2 * D_FF), D_MODEL),
        "ffn2_w_out": dense(ks[17], (L, D_FF, D_MODEL), D_FF),
    }


def reference(x, ffn1_norm, ffn1_w_in, ffn1_w_out, mix_norm, w_in, dn_conv_w, dn_a_log, dn_dt_bias,
              dn_out_norm, sb_q_norm, sb_k_norm, w_branch_a, w_branch_b, w_out,
              ffn2_norm, ffn2_w_in, ffn2_w_out):
    for l in range(DEPTH):
        x = x + FFN_HALF * swiglu(rmsnorm(x, ffn1_norm[l]), ffn1_w_in[l], ffn1_w_out[l])
        x = x + hybrid_mixer(rmsnorm(x, mix_norm[l]), w_in[l], dn_conv_w[l], dn_a_log[l], dn_dt_bias[l],
                             dn_out_norm[l], sb_q_norm[l], sb_k_norm[l],
                             w_branch_a[l], w_branch_b[l], w_out[l])
        x = x + FFN_HALF * swiglu(rmsnorm(x, ffn2_norm[l]), ffn2_w_in[l], ffn2_w_out[l])
    return x
```

```python
import functools

import jax
import jax.numpy as jnp
from jax import lax
from jax.experimental import pallas as pl
from jax.experimental.pallas import tpu as pltpu

F32 = jnp.float32
BF16 = jnp.bfloat16

HEAD_DIM = 128
DN_CONV = 4
DN_CHUNK = 64
RMS_EPS = 1e-6
L2_EPS = 1e-6
NEG_BIG = -1e30

MXU_DIM = 256
BLK = MXU_DIM
VMEM_LIMIT_BYTES = 56 * 1024 * 1024


def _dot(a, b):
    return jnp.dot(a, b, preferred_element_type=F32)


def _dot_nt(a, b):
    return lax.dot_general(a, b, (((1,), (1,)), ((), ())), preferred_element_type=F32)


def _dot_tn(a, b):
    return lax.dot_general(a, b, (((0,), (0,)), ((), ())), preferred_element_type=F32)


def _softplus(x):
    return jnp.maximum(x, 0.0) + jnp.log1p(jnp.exp(-jnp.abs(x)))


def _sigmoid(x):
    return 1.0 / (1.0 + jnp.exp(-x))


def _rms_rows(x, gain):
    ms = jnp.mean(x * x, axis=-1, keepdims=True)
    return x * lax.rsqrt(ms + RMS_EPS) * gain


def _split3(x):
    p0 = x.astype(BF16)
    r1 = x - p0.astype(F32)
    p1 = r1.astype(BF16)
    p2 = (r1 - p1.astype(F32)).astype(BF16)
    return p0, p1, p2


def _split2(x):
    p0 = x.astype(BF16)
    p1 = (x - p0.astype(F32)).astype(BF16)
    return p0, p1


def _resident(shape):
    return pl.BlockSpec(shape, lambda *_: (0,) * len(shape), pipeline_mode=pl.Buffered(1))


def _params(semantics):
    return pltpu.CompilerParams(dimension_semantics=semantics, vmem_limit_bytes=VMEM_LIMIT_BYTES)


def _ffn_body(x_ref, gain_ref, win_ref, wout_ref, o_ref, *, d_ff):
    x = x_ref[...]
    xn = _rms_rows(x, gain_ref[...]).astype(BF16)
    h = _dot(xn, win_ref[...])
    gate = h[:, :d_ff]
    up = h[:, d_ff:]
    act = (gate * _sigmoid(gate) * up).astype(BF16)
    o_ref[...] = x + 0.5 * _dot(act, wout_ref[...])


def _ffn(x, gain, w_in, w_out, *, tm):
    m, d = x.shape
    d_ff = w_out.shape[0]
    return pl.pallas_call(
        functools.partial(_ffn_body, d_ff=d_ff),
        grid=(m // tm,),
        in_specs=[
            pl.BlockSpec((tm, d), lambda i: (i, 0)),
            _resident((1, d)),
            _resident((d, 2 * d_ff)),
            _resident((d_ff, d)),
        ],
        out_specs=pl.BlockSpec((tm, d), lambda i: (i, 0)),
        out_shape=jax.ShapeDtypeStruct((m, d), F32),
        compiler_params=_params(("parallel",)),
        name="ffn",
    )(x, gain, w_in, w_out)


def _proj_body(x_ref, gain_ref, w_ref, wba_ref, qgain_ref, kgain_ref, avec_ref, dtvec_ref,
               dnqkv_ref, z_ref, qb_ref, kb_ref, vb_ref, ga_ref, gb_ref, ba_ref, *, d, heads):
    x = x_ref[...]
    xn = _rms_rows(x, gain_ref[...]).astype(BF16)
    dnqkv_ref[...] = _dot(xn, w_ref[:, 0:3 * d])
    z_ref[...] = _dot(xn, w_ref[:, 3 * d:4 * d])
    sq = _dot(xn, w_ref[:, 4 * d:5 * d])
    sk = _dot(xn, w_ref[:, 5 * d:6 * d])
    qgain = qgain_ref[...] * (HEAD_DIM ** -0.5)
    kgain = kgain_ref[...]
    for h in range(heads):
        sl = slice(h * HEAD_DIM, (h + 1) * HEAD_DIM)
        qb_ref[:, sl] = _rms_rows(sq[:, sl], qgain).astype(BF16)
        kb_ref[:, sl] = _rms_rows(sk[:, sl], kgain).astype(BF16)
    vb_ref[...] = _dot(xn, w_ref[:, 6 * d:7 * d]).astype(BF16)
    ga_ref[...] = _dot(xn, w_ref[:, 7 * d:8 * d])
    gb_ref[...] = _dot(xn, w_ref[:, 8 * d:9 * d])
    pba = _dot(xn, wba_ref[...])
    lane = lax.broadcasted_iota(jnp.int32, pba.shape, 1)
    beta = _sigmoid(pba)
    logdecay = avec_ref[...] * _softplus(pba + dtvec_ref[...])
    ba_ref[...] = jnp.where(lane < heads, beta, logdecay)


def _proj(x, gain, w, wba, qgain, kgain, avec, dtvec, *, tm, heads):
    m, d = x.shape
    row = lambda i: (i, 0)
    f32_out = lambda n: jax.ShapeDtypeStruct((m, n), F32)
    bf_out = lambda n: jax.ShapeDtypeStruct((m, n), BF16)
    return pl.pallas_call(
        functools.partial(_proj_body, d=d, heads=heads),
        grid=(m // tm,),
        in_specs=[
            pl.BlockSpec((tm, d), row),
            _resident((1, d)),
            _resident(w.shape),
            _resident(wba.shape),
            _resident((1, HEAD_DIM)),
            _resident((1, HEAD_DIM)),
            _resident((1, HEAD_DIM)),
            _resident((1, HEAD_DIM)),
        ],
        out_specs=[
            pl.BlockSpec((tm, 3 * d), row),
            pl.BlockSpec((tm, d), row),
            pl.BlockSpec((tm, d), row),
            pl.BlockSpec((tm, d), row),
            pl.BlockSpec((tm, d), row),
            pl.BlockSpec((tm, d), row),
            pl.BlockSpec((tm, d), row),
            pl.BlockSpec((tm, HEAD_DIM), row),
        ],
        out_shape=[f32_out(3 * d), f32_out(d), bf_out(d), bf_out(d), bf_out(d),
                   f32_out(d), f32_out(d), f32_out(HEAD_DIM)],
        compiler_params=_params(("parallel",)),
        name="proj",
    )(x, gain, w, wba, qgain, kgain, avec, dtvec)


def _shift_rows(x, tail, i):
    xs = pltpu.roll(x, i, 0)
    ts = pltpu.roll(tail, i, 0)
    row = lax.broadcasted_iota(jnp.int32, tail.shape, 0)
    head = jnp.where(row < i, ts, xs[:8])
    return jnp.concatenate([head, xs[8:]], axis=0)


def _conv_silu(x, tail, w):
    y = x * w[DN_CONV - 1:DN_CONV, :]
    for i in range(1, DN_CONV):
        y = y + _shift_rows(x, tail, i) * w[DN_CONV - 1 - i:DN_CONV - i, :]
    return y * _sigmoid(y)


def _column(tile, lane_idx):
    lane = lax.broadcasted_iota(jnp.int32, tile.shape, 1)
    return jnp.sum(jnp.where(lane == lane_idx, tile, 0.0), axis=1, keepdims=True)


def _unit_lower_inverse(lmat, eye):
    inv = eye - lmat
    power = lmat
    step = 1
    while 2 * step < DN_CHUNK + 1:
        pb = power.astype(BF16)
        power = _dot(pb, pb)
        inv = inv + _dot(inv.astype(BF16), power.astype(BF16))
        step *= 2
    return inv


def _dn_body(xq_ref, xk_ref, xv_ref, wq_ref, wk_ref, wv_ref, ba_ref, z_ref, ogain_ref,
             o_ref, state_ref, tail_ref, gct_ref, *, hg, heads):
    t = pl.program_id(2)
    group = pl.program_id(1)

    @pl.when(t == 0)
    def _():
        state_ref[...] = jnp.zeros_like(state_ref)
        tail_ref[...] = jnp.zeros_like(tail_ref)

    xq = xq_ref[...]
    xk = xk_ref[...]
    xv = xv_ref[...]
    qa = _conv_silu(xq, tail_ref[0], wq_ref[...])
    ka = _conv_silu(xk, tail_ref[1], wk_ref[...])
    va = _conv_silu(xv, tail_ref[2], wv_ref[...])
    tail_ref[0] = xq[BLK - 8:]
    tail_ref[1] = xk[BLK - 8:]
    tail_ref[2] = xv[BLK - 8:]

    row = lax.broadcasted_iota(jnp.int32, (BLK, BLK), 0)
    col = lax.broadcasted_iota(jnp.int32, (BLK, BLK), 1)
    same_chunk = (row // DN_CHUNK) == (col // DN_CHUNK)
    lower_incl = same_chunk & (row >= col)
    strict = same_chunk & (row > col)
    eye = jnp.where(row == col, 1.0, 0.0)
    cum_mat = jnp.where(lower_incl, 1.0, 0.0).astype(BF16)
    tot_mat = jnp.where(same_chunk, 1.0, 0.0).astype(BF16)

    ba = ba_ref[...]
    g0, g1, g2 = _split3(ba)
    gc_all = _dot(cum_mat, g0) + _dot(cum_mat, g1) + _dot(cum_mat, g2)
    gl_all = _dot(tot_mat, g0) + _dot(tot_mat, g1) + _dot(tot_mat, g2)
    gct_ref[...] = gc_all.T
    egc_all = jnp.exp(gc_all)
    ekd_all = jnp.exp(gl_all - gc_all)
    egl_all = jnp.exp(gl_all)

    ogain = ogain_ref[...]
    for j in range(hg):
        h = group * hg + j
        sl = slice(j * HEAD_DIM, (j + 1) * HEAD_DIM)
        q = qa[:, sl]
        k = ka[:, sl]
        v = va[:, sl]
        q = q * (lax.rsqrt(jnp.sum(q * q, axis=-1, keepdims=True) + L2_EPS) * (HEAD_DIM ** -0.5))
        k = k * lax.rsqrt(jnp.sum(k * k, axis=-1, keepdims=True) + L2_EPS)
        beta = _column(ba, h)
        gc = _column(gc_all, heads + h)
        egc = _column(egc_all, heads + h)
        ekd = _column(ekd_all, heads + h)
        egl = _column(egl_all, heads + h)
        gc_row = gct_ref[pl.ds(heads + h, 1), :]

        decay = jnp.exp(jnp.where(lower_incl, gc - gc_row, NEG_BIG))
        kb = k * beta
        kb16 = kb.astype(BF16)
        k16 = k.astype(BF16)
        lmat = jnp.where(strict, _dot_nt(kb16, k16) * decay, 0.0)
        inv = _unit_lower_inverse(lmat, eye)
        rhs = jnp.concatenate([v * beta, kb * egc], axis=1).astype(BF16)
        sol = _dot(inv.astype(BF16), rhs)
        u = sol[:, :HEAD_DIM]
        w16 = sol[:, HEAD_DIM:].astype(BF16)
        attn = _dot_nt(q.astype(BF16), k16) * decay
        qd16 = (q * egc).astype(BF16)
        kd16 = (k * ekd).astype(BF16)

        s = state_ref[j]
        vnews = []
        inters = []
        for c in range(BLK // DN_CHUNK):
            rows = slice(c * DN_CHUNK, (c + 1) * DN_CHUNK)
            lhs = jnp.concatenate([w16[rows], qd16[rows]], axis=0)
            ws_qs = _dot(lhs, s.astype(BF16))
            vnew = u[rows] - ws_qs[:DN_CHUNK]
            inters.append(ws_qs[DN_CHUNK:])
            vnews.append(vnew)
            e = egl[rows]
            scale = jnp.concatenate([e, e], axis=0)
            s = s * scale + _dot_tn(kd16[rows], vnew.astype(BF16))
        state_ref[j] = s
        vnew_all = jnp.concatenate(vnews, axis=0)
        o = jnp.concatenate(inters, axis=0) + _dot(attn.astype(BF16), vnew_all.astype(BF16))
        zt = z_ref[:, sl]
        o_ref[:, sl] = (_rms_rows(o, ogain) * (zt * _sigmoid(zt))).astype(BF16)


def _deltanet(dnqkv, conv_w, ba, z, ogain, *, batch, seq, heads, hg):
    m = dnqkv.shape[0]
    d = heads * HEAD_DIM
    nt = seq // BLK
    ng = heads // hg
    wide = hg * HEAD_DIM
    rows = lambda b, g, t: b * nt + t
    return pl.pallas_call(
        functools.partial(_dn_body, hg=hg, heads=heads),
        grid=(batch, ng, nt),
        in_specs=[
            pl.BlockSpec((BLK, wide), lambda b, g, t: (rows(b, g, t), g)),
            pl.BlockSpec((BLK, wide), lambda b, g, t: (rows(b, g, t), ng + g)),
            pl.BlockSpec((BLK, wide), lambda b, g, t: (rows(b, g, t), 2 * ng + g)),
            pl.BlockSpec((DN_CONV, wide), lambda b, g, t: (0, g)),
            pl.BlockSpec((DN_CONV, wide), lambda b, g, t: (0, ng + g)),
            pl.BlockSpec((DN_CONV, wide), lambda b, g, t: (0, 2 * ng + g)),
            pl.BlockSpec((BLK, HEAD_DIM), lambda b, g, t: (rows(b, g, t), 0)),
            pl.BlockSpec((BLK, wide), lambda b, g, t: (rows(b, g, t), g)),
            pl.BlockSpec((1, HEAD_DIM), lambda b, g, t: (0, 0)),
        ],
        out_specs=pl.BlockSpec((BLK, wide), lambda b, g, t: (rows(b, g, t), g)),
        out_shape=jax.ShapeDtypeStruct((m, d), BF16),
        scratch_shapes=[
            pltpu.VMEM((hg, HEAD_DIM, HEAD_DIM), F32),
            pltpu.VMEM((3, 8, wide), F32),
            pltpu.VMEM((HEAD_DIM, BLK), F32),
        ],
        compiler_params=_params(("parallel", "parallel", "arbitrary")),
        name="deltanet",
    )(dnqkv, dnqkv, dnqkv, conv_w, conv_w, conv_w, ba, z, ogain)


def _sb_body(q_ref, k_ref, v_ref, o_ref):
    qi = pl.program_id(2)
    q = q_ref[...]
    row = lax.broadcasted_iota(jnp.int32, (BLK, BLK), 0)
    col = lax.broadcasted_iota(jnp.int32, (BLK, BLK), 1)
    suffix = jnp.where(row >= col, 1.0, 0.0).astype(BF16)
    ones = jnp.ones((BLK, HEAD_DIM), BF16)

    def step(i, carry):
        acc, right = carry
        j = qi - i
        start = pl.multiple_of(j * BLK, BLK)
        kj = k_ref[pl.ds(start, BLK), :]
        vj = v_ref[pl.ds(start, BLK), :]
        z = _dot_nt(q, kj)
        causal = (col - row) < (i * BLK)
        sp = jnp.where(causal, _softplus(z), 0.0)
        hi, lo = _split2(sp)
        within = _dot(hi, suffix) + _dot(lo, suffix)
        total = _dot(hi, ones) + _dot(lo, ones)
        beyond = jnp.concatenate([right, right], axis=1)
        w = jnp.where(causal, jnp.exp(z - within - beyond), 0.0)
        acc = acc + _dot(w.astype(BF16), vj)
        return acc, right + total

    zeros = jnp.zeros((BLK, HEAD_DIM), F32)
    acc, _ = lax.fori_loop(0, qi + 1, step, (zeros, zeros))
    o_ref[...] = acc.astype(BF16)


def _stick_breaking(qb, kb, vb, *, batch, seq, heads):
    m, d = qb.shape
    nq = seq // BLK
    return pl.pallas_call(
        _sb_body,
        grid=(batch, heads, nq),
        in_specs=[
            pl.BlockSpec((BLK, HEAD_DIM), lambda b, h, i: (b * nq + i, h)),
            pl.BlockSpec((seq, HEAD_DIM), lambda b, h, i: (b, h)),
            pl.BlockSpec((seq, HEAD_DIM), lambda b, h, i: (b, h)),
        ],
        out_specs=pl.BlockSpec((BLK, HEAD_DIM), lambda b, h, i: (b * nq + i, h)),
        out_shape=jax.ShapeDtypeStruct((m, d), BF16),
        compiler_params=_params(("parallel", "parallel", "arbitrary")),
        name="stickbrk",
    )(qb, kb, vb)


def _merge_body(x_ref, oa_ref, ob_ref, ga_ref, gb_ref, wa_ref, wb_ref, wo_ref, o_ref):
    ya = _dot(oa_ref[...], wa_ref[...])
    yb = _dot(ob_ref[...], wb_ref[...])
    merged = _sigmoid(ga_ref[...]) * ya + _sigmoid(gb_ref[...]) * yb
    o_ref[...] = x_ref[...] + _dot(merged.astype(BF16), wo_ref[...])


def _merge(x, oa, ob, ga, gb, wa, wb, wo, *, tm):
    m, d = x.shape
    row = lambda i: (i, 0)
    tile = pl.BlockSpec((tm, d), row)
    return pl.pallas_call(
        _merge_body,
        grid=(m // tm,),
        in_specs=[tile, tile, tile, tile, tile, _resident((d, d)), _resident((d, d)), _resident((d, d))],
        out_specs=tile,
        out_shape=jax.ShapeDtypeStruct((m, d), F32),
        compiler_params=_params(("parallel",)),
        name="merge",
    )(x, oa, ob, ga, gb, wa, wb, wo)


def _row_tile(m, want):
    tm = min(want, m)
    assert m % tm == 0, (m, tm)
    return tm


def kernel(x, ffn1_norm, ffn1_w_in, ffn1_w_out, mix_norm, w_in, dn_conv_w, dn_a_log, dn_dt_bias,
           dn_out_norm, sb_q_norm, sb_k_norm, w_branch_a, w_branch_b, w_out,
           ffn2_norm, ffn2_w_in, ffn2_w_out):
    batch, seq, d = x.shape
    depth = ffn1_norm.shape[0]
    heads = d // HEAD_DIM
    assert seq % BLK == 0 and d % HEAD_DIM == 0 and 2 * heads <= HEAD_DIM
    m = batch * seq
    tm_ffn = _row_tile(m, 512)
    tm_proj = _row_tile(m, 256)
    tm_merge = _row_tile(m, 512)
    hg = 2 if heads % 2 == 0 else 1

    n_ba = 2 * heads
    w_main = jnp.concatenate([w_in[:, :, :4 * d], w_in[:, :, 4 * d + n_ba:]], axis=2).astype(BF16)
    w_ba = jnp.pad(w_in[:, :, 4 * d:4 * d + n_ba], ((0, 0), (0, 0), (0, HEAD_DIM - n_ba))).astype(BF16)
    lane_pad = lambda v: jnp.pad(v, ((0, 0), (heads, HEAD_DIM - n_ba)))[:, None, :]
    avec = lane_pad(-jnp.exp(dn_a_log.astype(F32)))
    dtvec = lane_pad(dn_dt_bias.astype(F32))
    f1_in, f1_out = ffn1_w_in.astype(BF16), ffn1_w_out.astype(BF16)
    f2_in, f2_out = ffn2_w_in.astype(BF16), ffn2_w_out.astype(BF16)
    wa, wb, wo = w_branch_a.astype(BF16), w_branch_b.astype(BF16), w_out.astype(BF16)

    xf = x.reshape(m, d)
    for l in range(depth):
        xf = _ffn(xf, ffn1_norm[l][None], f1_in[l], f1_out[l], tm=tm_ffn)
        dnqkv, z, qb, kb, vb, ga, gb, ba = _proj(
            xf, mix_norm[l][None], w_main[l], w_ba[l], sb_q_norm[l][None], sb_k_norm[l][None],
            avec[l], dtvec[l], tm=tm_proj, heads=heads)
        oa = _deltanet(dnqkv, dn_conv_w[l], ba, z, dn_out_norm[l][None],
                       batch=batch, seq=seq, heads=heads, hg=hg)
        ob = _stick_breaking(qb, kb, vb, batch=batch, seq=seq, heads=heads)
        xf = _merge(xf, oa, ob, ga, gb, wa[l], wb[l], wo[l], tm=tm_merge)
        xf = _ffn(xf, ffn2_norm[l][None], f2_in[l], f2_out[l], tm=tm_ffn)
    return xf.reshape(batch, seq, d)
```

```python
import functools

import jax
import jax.numpy as jnp
from jax import lax
from jax.experimental import pallas as pl
from jax.experimental.pallas import tpu as pltpu

F32 = jnp.float32
BF16 = jnp.bfloat16

HEAD_DIM = 128
DN_CONV = 4
INV_BASE = 8
RMS_EPS = 1e-6
L2_EPS = 1e-6
NEG_BIG = -1e30

MXU_DIM = 256
BLK = MXU_DIM
VMEM_LIMIT_BYTES = 56 * 1024 * 1024


def _dot(a, b):
    return jnp.dot(a, b, preferred_element_type=F32)


def _dot_nt(a, b):
    return lax.dot_general(a, b, (((1,), (1,)), ((), ())), preferred_element_type=F32)


def _dot_tn(a, b):
    return lax.dot_general(a, b, (((0,), (0,)), ((), ())), preferred_element_type=F32)


def _softplus(x):
    return jnp.maximum(x, 0.0) + jnp.log1p(jnp.exp(-jnp.abs(x)))


def _sigmoid(x):
    return 1.0 / (1.0 + jnp.exp(-x))


def _rms_rows(x, gain):
    ms = jnp.mean(x * x, axis=-1, keepdims=True)
    return x * lax.rsqrt(ms + RMS_EPS) * gain


def _split3(x):
    p0 = x.astype(BF16)
    r1 = x - p0.astype(F32)
    p1 = r1.astype(BF16)
    p2 = (r1 - p1.astype(F32)).astype(BF16)
    return p0, p1, p2


def _split2(x):
    p0 = x.astype(BF16)
    p1 = (x - p0.astype(F32)).astype(BF16)
    return p0, p1


def _resident(shape):
    return pl.BlockSpec(shape, lambda *_: (0,) * len(shape), pipeline_mode=pl.Buffered(1))


def _params(semantics):
    return pltpu.CompilerParams(dimension_semantics=semantics, vmem_limit_bytes=VMEM_LIMIT_BYTES)


def _ffn_body(x_ref, gain_ref, win_ref, wout_ref, o_ref, *, d_ff):
    x = x_ref[...]
    xn = _rms_rows(x, gain_ref[...]).astype(BF16)
    h = _dot(xn, win_ref[...])
    gate = h[:, :d_ff]
    up = h[:, d_ff:]
    act = (gate * _sigmoid(gate) * up).astype(BF16)
    o_ref[...] = x + 0.5 * _dot(act, wout_ref[...])


def _ffn(x, gain, w_in, w_out, *, tm):
    m, d = x.shape
    d_ff = w_out.shape[0]
    return pl.pallas_call(
        functools.partial(_ffn_body, d_ff=d_ff),
        grid=(m // tm,),
        in_specs=[
            pl.BlockSpec((tm, d), lambda i: (i, 0)),
            _resident((1, d)),
            _resident((d, 2 * d_ff)),
            _resident((d_ff, d)),
        ],
        out_specs=pl.BlockSpec((tm, d), lambda i: (i, 0)),
        out_shape=jax.ShapeDtypeStruct((m, d), F32),
        compiler_params=_params(("parallel",)),
        name="ffn",
    )(x, gain, w_in, w_out)


def _proj_body(x_ref, gain_ref, w_ref, wba_ref, qgain_ref, kgain_ref, avec_ref, dtvec_ref,
               dnqkv_ref, z_ref, qb_ref, kb_ref, vb_ref, ga_ref, gb_ref, ba_ref, *, d, heads):
    x = x_ref[...]
    xn = _rms_rows(x, gain_ref[...]).astype(BF16)
    dnqkv_ref[...] = _dot(xn, w_ref[:, 0:3 * d])
    z_ref[...] = _dot(xn, w_ref[:, 3 * d:4 * d])
    sq = _dot(xn, w_ref[:, 4 * d:5 * d])
    sk = _dot(xn, w_ref[:, 5 * d:6 * d])
    qgain = qgain_ref[...] * (HEAD_DIM ** -0.5)
    kgain = kgain_ref[...]
    for h in range(heads):
        sl = slice(h * HEAD_DIM, (h + 1) * HEAD_DIM)
        qb_ref[:, sl] = _rms_rows(sq[:, sl], qgain).astype(BF16)
        kb_ref[:, sl] = _rms_rows(sk[:, sl], kgain).astype(BF16)
    vb_ref[...] = _dot(xn, w_ref[:, 6 * d:7 * d]).astype(BF16)
    ga_ref[...] = _dot(xn, w_ref[:, 7 * d:8 * d])
    gb_ref[...] = _dot(xn, w_ref[:, 8 * d:9 * d])
    pba = _dot(xn, wba_ref[...])
    lane = lax.broadcasted_iota(jnp.int32, pba.shape, 1)
    beta = _sigmoid(pba)
    logdecay = avec_ref[...] * _softplus(pba + dtvec_ref[...])
    ba_ref[...] = jnp.where(lane < heads, beta, logdecay)


def _proj(x, gain, w, wba, qgain, kgain, avec, dtvec, *, tm, heads):
    m, d = x.shape
    row = lambda i: (i, 0)
    f32_out = lambda n: jax.ShapeDtypeStruct((m, n), F32)
    bf_out = lambda n: jax.ShapeDtypeStruct((m, n), BF16)
    return pl.pallas_call(
        functools.partial(_proj_body, d=d, heads=heads),
        grid=(m // tm,),
        in_specs=[
            pl.BlockSpec((tm, d), row),
            _resident((1, d)),
            _resident(w.shape),
            _resident(wba.shape),
            _resident((1, HEAD_DIM)),
            _resident((1, HEAD_DIM)),
            _resident((1, HEAD_DIM)),
            _resident((1, HEAD_DIM)),
        ],
        out_specs=[
            pl.BlockSpec((tm, 3 * d), row),
            pl.BlockSpec((tm, d), row),
            pl.BlockSpec((tm, d), row),
            pl.BlockSpec((tm, d), row),
            pl.BlockSpec((tm, d), row),
            pl.BlockSpec((tm, d), row),
            pl.BlockSpec((tm, d), row),
            pl.BlockSpec((tm, HEAD_DIM), row),
        ],
        out_shape=[f32_out(3 * d), f32_out(d), bf_out(d), bf_out(d), bf_out(d),
                   f32_out(d), f32_out(d), f32_out(HEAD_DIM)],
        compiler_params=_params(("parallel",)),
        name="proj",
    )(x, gain, w, wba, qgain, kgain, avec, dtvec)


def _shift_rows(x, tail, i):
    xs = pltpu.roll(x, i, 0)
    ts = pltpu.roll(tail, i, 0)
    row = lax.broadcasted_iota(jnp.int32, tail.shape, 0)
    head = jnp.where(row < i, ts, xs[:8])
    return jnp.concatenate([head, xs[8:]], axis=0)


def _conv_silu(x, tail, w):
    y = x * w[DN_CONV - 1:DN_CONV, :]
    for i in range(1, DN_CONV):
        y = y + _shift_rows(x, tail, i) * w[DN_CONV - 1 - i:DN_CONV - i, :]
    return y * _sigmoid(y)


def _column(tile, lane_idx):
    lane = lax.broadcasted_iota(jnp.int32, tile.shape, 1)
    return jnp.sum(jnp.where(lane == lane_idx, tile, 0.0), axis=1, keepdims=True)


def _inverse_masks(row, col):
    shift = INV_BASE.bit_length() - 1
    blk = lambda x, s: lax.shift_right_logical(x, s)
    base = blk(row, shift) == blk(col, shift)
    joins = []
    while (1 << shift) < BLK:
        joins.append((blk(row, shift + 1) == blk(col, shift + 1)) & (blk(row, shift) != blk(col, shift)))
        shift += 1
    return base, joins


def _unit_lower_inverse(lmats, eye, masks):
    base, joins = masks
    blocks = [jnp.where(base, l, 0.0) for l in lmats]
    invs = [eye - b for b in blocks]
    powers = [b.astype(BF16) for b in blocks]
    span = 2
    while span < INV_BASE:
        powers = [_dot(p, p).astype(BF16) for p in powers]
        invs = [t + _dot(t.astype(BF16), p) for t, p in zip(invs, powers)]
        span *= 2
    for join in joins:
        t16s = [t.astype(BF16) for t in invs]
        cts = [_dot(jnp.where(join, l, 0.0).astype(BF16), t16).astype(BF16) for l, t16 in zip(lmats, t16s)]
        invs = [t - _dot(t16, ct) for t, t16, ct in zip(invs, t16s, cts)]
    return invs


def _dn_body(xq_ref, xk_ref, xv_ref, wq_ref, wk_ref, wv_ref, ba_ref, z_ref, ogain_ref,
             o_ref, state_ref, tail_ref, gct_ref, *, hg, heads):
    t = pl.program_id(2)
    group = pl.program_id(1)

    @pl.when(t == 0)
    def _():
        state_ref[...] = jnp.zeros_like(state_ref)
        tail_ref[...] = jnp.zeros_like(tail_ref)

    xq = xq_ref[...]
    xk = xk_ref[...]
    xv = xv_ref[...]
    qa = _conv_silu(xq, tail_ref[0], wq_ref[...])
    ka = _conv_silu(xk, tail_ref[1], wk_ref[...])
    va = _conv_silu(xv, tail_ref[2], wv_ref[...])
    tail_ref[0] = xq[BLK - 8:]
    tail_ref[1] = xk[BLK - 8:]
    tail_ref[2] = xv[BLK - 8:]

    row = lax.broadcasted_iota(jnp.int32, (BLK, BLK), 0)
    col = lax.broadcasted_iota(jnp.int32, (BLK, BLK), 1)
    lower_incl = row >= col
    strict = row > col
    eye = jnp.where(row == col, 1.0, 0.0)
    masks = _inverse_masks(row, col)
    cum_mat = jnp.where(lower_incl, 1.0, 0.0).astype(BF16)

    ba = ba_ref[...]
    g0, g1, g2 = _split3(ba)
    gc_all = _dot(cum_mat, g0) + _dot(cum_mat, g1) + _dot(cum_mat, g2)
    gl_all = gc_all[BLK - 1:BLK, :]
    gct_ref[...] = gc_all.T
    egc_all = jnp.exp(gc_all)
    ekd_all = jnp.exp(gl_all - gc_all)
    egl_all = jnp.broadcast_to(jnp.exp(gl_all), (HEAD_DIM, HEAD_DIM))

    ogain = ogain_ref[...]
    js = range(hg)
    lanes = [slice(j * HEAD_DIM, (j + 1) * HEAD_DIM) for j in js]
    hs = [group * hg + j for j in js]
    l2 = lambda x: lax.rsqrt(jnp.sum(x * x, axis=-1, keepdims=True) + L2_EPS)
    qs = [qa[:, sl] * (l2(qa[:, sl]) * (HEAD_DIM ** -0.5)) for sl in lanes]
    ks = [ka[:, sl] * l2(ka[:, sl]) for sl in lanes]
    betas = [_column(ba, h) for h in hs]
    egcs = [_column(egc_all, heads + h) for h in hs]
    ekds = [_column(ekd_all, heads + h) for h in hs]
    egls = [_column(egl_all, heads + h) for h in hs]
    decays = [jnp.exp(jnp.where(lower_incl,
                                _column(gc_all, heads + h) - gct_ref[pl.ds(heads + h, 1), :], NEG_BIG))
              for h in hs]
    kbs = [k * b for k, b in zip(ks, betas)]
    k16s = [k.astype(BF16) for k in ks]
    lmats = [jnp.where(strict, _dot_nt(kb.astype(BF16), k16) * dec, 0.0)
             for kb, k16, dec in zip(kbs, k16s, decays)]
    invs = _unit_lower_inverse(lmats, eye, masks)
    rhss = [jnp.concatenate([va[:, sl] * b, kb * egc], axis=1).astype(BF16)
            for sl, b, kb, egc in zip(lanes, betas, kbs, egcs)]
    sols = [_dot(inv.astype(BF16), rhs) for inv, rhs in zip(invs, rhss)]
    attn16s = [(_dot_nt(q.astype(BF16), k16) * dec).astype(BF16) for q, k16, dec in zip(qs, k16s, decays)]
    qd16s = [(q * egc).astype(BF16) for q, egc in zip(qs, egcs)]
    kd16s = [(k * ekd).astype(BF16) for k, ekd in zip(ks, ekds)]
    states = [state_ref[j] for j in js]
    ws_qss = [_dot(jnp.concatenate([sol[:, HEAD_DIM:].astype(BF16), qd16], axis=0), s.astype(BF16))
              for sol, qd16, s in zip(sols, qd16s, states)]
    vnew16s = [(sol[:, :HEAD_DIM] - ws_qs[:BLK]).astype(BF16) for sol, ws_qs in zip(sols, ws_qss)]
    outs = [ws_qs[BLK:] + _dot(attn16, vnew16) for ws_qs, attn16, vnew16 in zip(ws_qss, attn16s, vnew16s)]
    for j, s, egl, kd16, vnew16 in zip(js, states, egls, kd16s, vnew16s):
        state_ref[j] = s * egl + _dot_tn(kd16, vnew16)
    for sl, o in zip(lanes, outs):
        zt = z_ref[:, sl]
        o_ref[:, sl] = (_rms_rows(o, ogain) * (zt * _sigmoid(zt))).astype(BF16)


def _deltanet(dnqkv, conv_w, ba, z, ogain, *, batch, seq, heads, hg):
    m = dnqkv.shape[0]
    d = heads * HEAD_DIM
    nt = seq // BLK
    ng = heads // hg
    wide = hg * HEAD_DIM
    rows = lambda b, g, t: b * nt + t
    return pl.pallas_call(
        functools.partial(_dn_body, hg=hg, heads=heads),
        grid=(batch, ng, nt),
        in_specs=[
            pl.BlockSpec((BLK, wide), lambda b, g, t: (rows(b, g, t), g)),
            pl.BlockSpec((BLK, wide), lambda b, g, t: (rows(b, g, t), ng + g)),
            pl.BlockSpec((BLK, wide), lambda b, g, t: (rows(b, g, t), 2 * ng + g)),
            pl.BlockSpec((DN_CONV, wide), lambda b, g, t: (0, g)),
            pl.BlockSpec((DN_CONV, wide), lambda b, g, t: (0, ng + g)),
            pl.BlockSpec((DN_CONV, wide), lambda b, g, t: (0, 2 * ng + g)),
            pl.BlockSpec((BLK, HEAD_DIM), lambda b, g, t: (rows(b, g, t), 0)),
            pl.BlockSpec((BLK, wide), lambda b, g, t: (rows(b, g, t), g)),
            pl.BlockSpec((1, HEAD_DIM), lambda b, g, t: (0, 0)),
        ],
        out_specs=pl.BlockSpec((BLK, wide), lambda b, g, t: (rows(b, g, t), g)),
        out_shape=jax.ShapeDtypeStruct((m, d), BF16),
        scratch_shapes=[
            pltpu.VMEM((hg, HEAD_DIM, HEAD_DIM), F32),
            pltpu.VMEM((3, 8, wide), F32),
            pltpu.VMEM((HEAD_DIM, BLK), F32),
        ],
        compiler_params=_params(("parallel", "parallel", "arbitrary")),
        name="deltanet",
    )(dnqkv, dnqkv, dnqkv, conv_w, conv_w, conv_w, ba, z, ogain)


def _sb_tiles(qs, kjs, vjs, carry, suffix2, causal):
    zs = [_dot_nt(q, kj) for q, kj in zip(qs, kjs)]
    sps = [_softplus(z) for z in zs]
    if causal is not None:
        sps = [jnp.where(causal, sp, 0.0) for sp in sps]
    cums = [_dot(jnp.concatenate(_split2(sp), axis=1), suffix2) for sp in sps]
    ws = [jnp.exp(z - cum - jnp.concatenate([right, right], axis=1))
          for z, cum, (_, right) in zip(zs, cums, carry)]
    if causal is not None:
        ws = [jnp.where(causal, w, 0.0) for w in ws]
    return tuple((acc + _dot(w.astype(BF16), vj), right + jnp.broadcast_to(cum[:, 0:1], right.shape))
                 for w, vj, cum, (acc, right) in zip(ws, vjs, cums, carry))


def _sb_body(q_ref, k_ref, v_ref, o_ref, *, hs):
    qi = pl.program_id(2)
    row = lax.broadcasted_iota(jnp.int32, (BLK, BLK), 0)
    col = lax.broadcasted_iota(jnp.int32, (BLK, BLK), 1)
    suffix = jnp.where(row >= col, 1.0, 0.0).astype(BF16)
    suffix2 = jnp.concatenate([suffix, suffix], axis=0)
    lanes = [slice(h * HEAD_DIM, (h + 1) * HEAD_DIM) for h in range(hs)]
    qs = [q_ref[:, sl] for sl in lanes]

    def sweep(start, causal, carry):
        kjs = [k_ref[pl.ds(start, BLK), sl] for sl in lanes]
        vjs = [v_ref[pl.ds(start, BLK), sl] for sl in lanes]
        return _sb_tiles(qs, kjs, vjs, carry, suffix2, causal)

    zeros = jnp.zeros((BLK, HEAD_DIM), F32)
    carry = sweep(pl.multiple_of(qi * BLK, BLK), col < row, tuple((zeros, zeros) for _ in lanes))

    def step(i, carry):
        return sweep(pl.multiple_of((qi - 1 - i) * BLK, BLK), None, carry)

    carry = lax.fori_loop(0, qi, step, carry)
    for h, sl in enumerate(lanes):
        o_ref[:, sl] = carry[h][0].astype(BF16)


def _stick_breaking(qb, kb, vb, *, batch, seq, heads, hs):
    m, d = qb.shape
    nq = seq // BLK
    wide = hs * HEAD_DIM
    return pl.pallas_call(
        functools.partial(_sb_body, hs=hs),
        grid=(batch, heads // hs, nq),
        in_specs=[
            pl.BlockSpec((BLK, wide), lambda b, h, i: (b * nq + i, h)),
            pl.BlockSpec((seq, wide), lambda b, h, i: (b, h)),
            pl.BlockSpec((seq, wide), lambda b, h, i: (b, h)),
        ],
        out_specs=pl.BlockSpec((BLK, wide), lambda b, h, i: (b * nq + i, h)),
        out_shape=jax.ShapeDtypeStruct((m, d), BF16),
        compiler_params=_params(("parallel", "parallel", "arbitrary")),
        name="stickbrk",
    )(qb, kb, vb)


def _merge_body(x_ref, oa_ref, ob_ref, ga_ref, gb_ref, wa_ref, wb_ref, wo_ref, o_ref):
    ya = _dot(oa_ref[...], wa_ref[...])
    yb = _dot(ob_ref[...], wb_ref[...])
    merged = _sigmoid(ga_ref[...]) * ya + _sigmoid(gb_ref[...]) * yb
    o_ref[...] = x_ref[...] + _dot(merged.astype(BF16), wo_ref[...])


def _merge(x, oa, ob, ga, gb, wa, wb, wo, *, tm):
    m, d = x.shape
    row = lambda i: (i, 0)
    tile = pl.BlockSpec((tm, d), row)
    return pl.pallas_call(
        _merge_body,
        grid=(m // tm,),
        in_specs=[tile, tile, tile, tile, tile, _resident((d, d)), _resident((d, d)), _resident((d, d))],
        out_specs=tile,
        out_shape=jax.ShapeDtypeStruct((m, d), F32),
        compiler_params=_params(("parallel",)),
        name="merge",
    )(x, oa, ob, ga, gb, wa, wb, wo)


def _row_tile(m, want):
    tm = min(want, m)
    assert m % tm == 0, (m, tm)
    return tm


def kernel(x, ffn1_norm, ffn1_w_in, ffn1_w_out, mix_norm, w_in, dn_conv_w, dn_a_log, dn_dt_bias,
           dn_out_norm, sb_q_norm, sb_k_norm, w_branch_a, w_branch_b, w_out,
           ffn2_norm, ffn2_w_in, ffn2_w_out):
    batch, seq, d = x.shape
    depth = ffn1_norm.shape[0]
    heads = d // HEAD_DIM
    assert seq % BLK == 0 and d % HEAD_DIM == 0 and 2 * heads <= HEAD_DIM
    m = batch * seq
    tm_ffn = _row_tile(m, 512)
    tm_proj = _row_tile(m, 256)
    tm_merge = _row_tile(m, 512)
    hg = 4 if heads % 4 == 0 else 1
    hs = 2 if heads % 2 == 0 else 1

    n_ba = 2 * heads
    w_main = jnp.concatenate([w_in[:, :, :4 * d], w_in[:, :, 4 * d + n_ba:]], axis=2).astype(BF16)
    w_ba = jnp.pad(w_in[:, :, 4 * d:4 * d + n_ba], ((0, 0), (0, 0), (0, HEAD_DIM - n_ba))).astype(BF16)
    lane_pad = lambda v: jnp.pad(v, ((0, 0), (heads, HEAD_DIM - n_ba)))[:, None, :]
    avec = lane_pad(-jnp.exp(dn_a_log.astype(F32)))
    dtvec = lane_pad(dn_dt_bias.astype(F32))
    f1_in, f1_out = ffn1_w_in.astype(BF16), ffn1_w_out.astype(BF16)
    f2_in, f2_out = ffn2_w_in.astype(BF16), ffn2_w_out.astype(BF16)
    wa, wb, wo = w_branch_a.astype(BF16), w_branch_b.astype(BF16), w_out.astype(BF16)

    xf = x.reshape(m, d)
    for l in range(depth):
        xf = _ffn(xf, ffn1_norm[l][None], f1_in[l], f1_out[l], tm=tm_ffn)
        dnqkv, z, qb, kb, vb, ga, gb, ba = _proj(
            xf, mix_norm[l][None], w_main[l], w_ba[l], sb_q_norm[l][None], sb_k_norm[l][None],
            avec[l], dtvec[l], tm=tm_proj, heads=heads)
        oa = _deltanet(dnqkv, dn_conv_w[l], ba, z, dn_out_norm[l][None],
                       batch=batch, seq=seq, heads=heads, hg=hg)
        ob = _stick_breaking(qb, kb, vb, batch=batch, seq=seq, heads=heads, hs=hs)
        xf = _merge(xf, oa, ob, ga, gb, wa[l], wb[l], wo[l], tm=tm_merge)
        xf = _ffn(xf, ffn2_norm[l][None], f2_in[l], f2_out[l], tm=tm_ffn)
    return xf.reshape(batch, seq, d)
```

```python
import functools

import jax
import jax.numpy as jnp
from jax import lax
from jax.experimental import pallas as pl
from jax.experimental.pallas import tpu as pltpu

F32 = jnp.float32
BF16 = jnp.bfloat16

HEAD_DIM = 128
DN_CONV = 4
INV_BASE = 8
RMS_EPS = 1e-6
L2_EPS = 1e-6
NEG_BIG = -1e30

MXU_DIM = 256
BLK = MXU_DIM
VMEM_LIMIT_BYTES = 56 * 1024 * 1024


def _dot(a, b):
    return jnp.dot(a, b, preferred_element_type=F32)


def _dot_nt(a, b):
    return lax.dot_general(a, b, (((1,), (1,)), ((), ())), preferred_element_type=F32)


def _dot_tn(a, b):
    return lax.dot_general(a, b, (((0,), (0,)), ((), ())), preferred_element_type=F32)


def _softplus(x):
    return jnp.maximum(x, 0.0) + jnp.log1p(jnp.exp(-jnp.abs(x)))


LOG2E = 1.4426950408889634


def _softplus_fast(x):
    one = jnp.ones((), x.dtype)
    return jnp.maximum(x, 0 * one) + jnp.log(one + jnp.exp2(jnp.abs(x) * (-LOG2E * one)))


def _sigmoid(x):
    return 1.0 / (1.0 + jnp.exp(-x))


def _rms_rows(x, gain):
    ms = jnp.mean(x * x, axis=-1, keepdims=True)
    return x * lax.rsqrt(ms + RMS_EPS) * gain


def _split3(x):
    p0 = x.astype(BF16)
    r1 = x - p0.astype(F32)
    p1 = r1.astype(BF16)
    p2 = (r1 - p1.astype(F32)).astype(BF16)
    return p0, p1, p2


def _split2(x):
    p0 = x.astype(BF16)
    p1 = (x - p0.astype(F32)).astype(BF16)
    return p0, p1


def _resident(shape):
    return pl.BlockSpec(shape, lambda *_: (0,) * len(shape), pipeline_mode=pl.Buffered(1))


def _params(semantics):
    return pltpu.CompilerParams(dimension_semantics=semantics, vmem_limit_bytes=VMEM_LIMIT_BYTES)


def _ffn_body(x_ref, gain_ref, win_ref, wout_ref, o_ref, *, d_ff):
    x = x_ref[...]
    xn = _rms_rows(x, gain_ref[...]).astype(BF16)
    h = _dot(xn, win_ref[...])
    gate = h[:, :d_ff]
    up = h[:, d_ff:]
    act = (gate * _sigmoid(gate) * up).astype(BF16)
    o_ref[...] = x + 0.5 * _dot(act, wout_ref[...])


def _ffn(x, gain, w_in, w_out, *, tm):
    m, d = x.shape
    d_ff = w_out.shape[0]
    return pl.pallas_call(
        functools.partial(_ffn_body, d_ff=d_ff),
        grid=(m // tm,),
        in_specs=[
            pl.BlockSpec((tm, d), lambda i: (i, 0)),
            _resident((1, d)),
            _resident((d, 2 * d_ff)),
            _resident((d_ff, d)),
        ],
        out_specs=pl.BlockSpec((tm, d), lambda i: (i, 0)),
        out_shape=jax.ShapeDtypeStruct((m, d), F32),
        compiler_params=_params(("parallel",)),
        name="ffn",
    )(x, gain, w_in, w_out)


def _proj_body(x_ref, gain_ref, w_ref, wba_ref, qgain_ref, kgain_ref, avec_ref, dtvec_ref,
               dnqkv_ref, z_ref, qb_ref, kb_ref, vb_ref, ga_ref, gb_ref, ba_ref, *, d, heads):
    x = x_ref[...]
    xn = _rms_rows(x, gain_ref[...]).astype(BF16)
    dnqkv_ref[...] = _dot(xn, w_ref[:, 0:3 * d])
    z_ref[...] = _dot(xn, w_ref[:, 3 * d:4 * d])
    sq = _dot(xn, w_ref[:, 4 * d:5 * d])
    sk = _dot(xn, w_ref[:, 5 * d:6 * d])
    qgain = qgain_ref[...] * (HEAD_DIM ** -0.5)
    kgain = kgain_ref[...]
    for h in range(heads):
        sl = slice(h * HEAD_DIM, (h + 1) * HEAD_DIM)
        qb_ref[:, sl] = _rms_rows(sq[:, sl], qgain).astype(BF16)
        kb_ref[:, sl] = _rms_rows(sk[:, sl], kgain).astype(BF16)
    vb_ref[...] = _dot(xn, w_ref[:, 6 * d:7 * d]).astype(BF16)
    ga_ref[...] = _dot(xn, w_ref[:, 7 * d:8 * d])
    gb_ref[...] = _dot(xn, w_ref[:, 8 * d:9 * d])
    pba = _dot(xn, wba_ref[...])
    lane = lax.broadcasted_iota(jnp.int32, pba.shape, 1)
    beta = _sigmoid(pba)
    logdecay = avec_ref[...] * _softplus(pba + dtvec_ref[...])
    ba_ref[...] = jnp.where(lane < heads, beta, logdecay)


def _proj(x, gain, w, wba, qgain, kgain, avec, dtvec, *, tm, heads):
    m, d = x.shape
    row = lambda i: (i, 0)
    f32_out = lambda n: jax.ShapeDtypeStruct((m, n), F32)
    bf_out = lambda n: jax.ShapeDtypeStruct((m, n), BF16)
    return pl.pallas_call(
        functools.partial(_proj_body, d=d, heads=heads),
        grid=(m // tm,),
        in_specs=[
            pl.BlockSpec((tm, d), row),
            _resident((1, d)),
            _resident(w.shape),
            _resident(wba.shape),
            _resident((1, HEAD_DIM)),
            _resident((1, HEAD_DIM)),
            _resident((1, HEAD_DIM)),
            _resident((1, HEAD_DIM)),
        ],
        out_specs=[
            pl.BlockSpec((tm, 3 * d), row),
            pl.BlockSpec((tm, d), row),
            pl.BlockSpec((tm, d), row),
            pl.BlockSpec((tm, d), row),
            pl.BlockSpec((tm, d), row),
            pl.BlockSpec((tm, d), row),
            pl.BlockSpec((tm, d), row),
            pl.BlockSpec((tm, HEAD_DIM), row),
        ],
        out_shape=[f32_out(3 * d), f32_out(d), bf_out(d), bf_out(d), bf_out(d),
                   f32_out(d), f32_out(d), f32_out(HEAD_DIM)],
        compiler_params=_params(("parallel",)),
        name="proj",
    )(x, gain, w, wba, qgain, kgain, avec, dtvec)


def _shift_rows(x, tail, i):
    xs = pltpu.roll(x, i, 0)
    ts = pltpu.roll(tail, i, 0)
    row = lax.broadcasted_iota(jnp.int32, tail.shape, 0)
    head = jnp.where(row < i, ts, xs[:8])
    return jnp.concatenate([head, xs[8:]], axis=0)


def _conv_silu(x, tail, w):
    y = x * w[DN_CONV - 1:DN_CONV, :]
    for i in range(1, DN_CONV):
        y = y + _shift_rows(x, tail, i) * w[DN_CONV - 1 - i:DN_CONV - i, :]
    return y * _sigmoid(y)


def _column(tile, lane_idx):
    lane = lax.broadcasted_iota(jnp.int32, tile.shape, 1)
    return jnp.sum(jnp.where(lane == lane_idx, tile, 0.0), axis=1, keepdims=True)


def _inverse_masks(row, col):
    shift = INV_BASE.bit_length() - 1
    blk = lambda x, s: lax.shift_right_logical(x, s)
    base = blk(row, shift) == blk(col, shift)
    joins = []
    while (1 << shift) < BLK:
        joins.append((blk(row, shift + 1) == blk(col, shift + 1)) & (blk(row, shift) != blk(col, shift)))
        shift += 1
    return base, joins


def _unit_lower_inverse(lmats, eye, masks):
    base, joins = masks
    blocks = [jnp.where(base, l, 0.0) for l in lmats]
    invs = [eye - b for b in blocks]
    powers = [b.astype(BF16) for b in blocks]
    span = 2
    while span < INV_BASE:
        powers = [_dot(p, p).astype(BF16) for p in powers]
        invs = [t + _dot(t.astype(BF16), p) for t, p in zip(invs, powers)]
        span *= 2
    for join in joins:
        t16s = [t.astype(BF16) for t in invs]
        cts = [_dot(jnp.where(join, l, 0.0).astype(BF16), t16).astype(BF16) for l, t16 in zip(lmats, t16s)]
        invs = [t - _dot(t16, ct) for t, t16, ct in zip(invs, t16s, cts)]
    return invs


def _dn_body(xq_ref, xk_ref, xv_ref, wq_ref, wk_ref, wv_ref, ba_ref, z_ref, ogain_ref,
             o_ref, state_ref, tail_ref, gct_ref, *, hg, heads):
    t = pl.program_id(2)
    group = pl.program_id(1)

    @pl.when(t == 0)
    def _():
        state_ref[...] = jnp.zeros_like(state_ref)
        tail_ref[...] = jnp.zeros_like(tail_ref)

    xq = xq_ref[...]
    xk = xk_ref[...]
    xv = xv_ref[...]
    qa = _conv_silu(xq, tail_ref[0], wq_ref[...])
    ka = _conv_silu(xk, tail_ref[1], wk_ref[...])
    va = _conv_silu(xv, tail_ref[2], wv_ref[...])
    tail_ref[0] = xq[BLK - 8:]
    tail_ref[1] = xk[BLK - 8:]
    tail_ref[2] = xv[BLK - 8:]

    row = lax.broadcasted_iota(jnp.int32, (BLK, BLK), 0)
    col = lax.broadcasted_iota(jnp.int32, (BLK, BLK), 1)
    lower_incl = row >= col
    strict = row > col
    eye = jnp.where(row == col, 1.0, 0.0)
    masks = _inverse_masks(row, col)
    cum_mat = jnp.where(lower_incl, 1.0, 0.0).astype(BF16)

    ba = ba_ref[...]
    g0, g1, g2 = _split3(ba)
    gc_all = _dot(cum_mat, g0) + _dot(cum_mat, g1) + _dot(cum_mat, g2)
    gl_all = gc_all[BLK - 1:BLK, :]
    gct_ref[...] = gc_all.T
    egc_all = jnp.exp(gc_all)
    ekd_all = jnp.exp(gl_all - gc_all)
    egl_all = jnp.broadcast_to(jnp.exp(gl_all), (HEAD_DIM, HEAD_DIM))

    ogain = ogain_ref[...]
    js = range(hg)
    lanes = [slice(j * HEAD_DIM, (j + 1) * HEAD_DIM) for j in js]
    hs = [group * hg + j for j in js]
    l2 = lambda x: lax.rsqrt(jnp.sum(x * x, axis=-1, keepdims=True) + L2_EPS)
    qs = [qa[:, sl] * (l2(qa[:, sl]) * (HEAD_DIM ** -0.5)) for sl in lanes]
    ks = [ka[:, sl] * l2(ka[:, sl]) for sl in lanes]
    betas = [_column(ba, h) for h in hs]
    egcs = [_column(egc_all, heads + h) for h in hs]
    ekds = [_column(ekd_all, heads + h) for h in hs]
    egls = [_column(egl_all, heads + h) for h in hs]
    decays = [jnp.exp(jnp.where(lower_incl,
                                _column(gc_all, heads + h) - gct_ref[pl.ds(heads + h, 1), :], NEG_BIG))
              for h in hs]
    kbs = [k * b for k, b in zip(ks, betas)]
    k16s = [k.astype(BF16) for k in ks]
    lmats = [jnp.where(strict, _dot_nt(kb.astype(BF16), k16) * dec, 0.0)
             for kb, k16, dec in zip(kbs, k16s, decays)]
    invs = _unit_lower_inverse(lmats, eye, masks)
    rhss = [jnp.concatenate([va[:, sl] * b, kb * egc], axis=1).astype(BF16)
            for sl, b, kb, egc in zip(lanes, betas, kbs, egcs)]
    sols = [_dot(inv.astype(BF16), rhs) for inv, rhs in zip(invs, rhss)]
    attn16s = [(_dot_nt(q.astype(BF16), k16) * dec).astype(BF16) for q, k16, dec in zip(qs, k16s, decays)]
    qd16s = [(q * egc).astype(BF16) for q, egc in zip(qs, egcs)]
    kd16s = [(k * ekd).astype(BF16) for k, ekd in zip(ks, ekds)]
    states = [state_ref[j] for j in js]
    ws_qss = [_dot(jnp.concatenate([sol[:, HEAD_DIM:].astype(BF16), qd16], axis=0), s.astype(BF16))
              for sol, qd16, s in zip(sols, qd16s, states)]
    vnew16s = [(sol[:, :HEAD_DIM] - ws_qs[:BLK]).astype(BF16) for sol, ws_qs in zip(sols, ws_qss)]
    outs = [ws_qs[BLK:] + _dot(attn16, vnew16) for ws_qs, attn16, vnew16 in zip(ws_qss, attn16s, vnew16s)]
    for j, s, egl, kd16, vnew16 in zip(js, states, egls, kd16s, vnew16s):
        state_ref[j] = s * egl + _dot_tn(kd16, vnew16)
    for sl, o in zip(lanes, outs):
        zt = z_ref[:, sl]
        o_ref[:, sl] = (_rms_rows(o, ogain) * (zt * _sigmoid(zt))).astype(BF16)


def _deltanet(dnqkv, conv_w, ba, z, ogain, *, batch, seq, heads, hg):
    m = dnqkv.shape[0]
    d = heads * HEAD_DIM
    nt = seq // BLK
    ng = heads // hg
    wide = hg * HEAD_DIM
    rows = lambda b, g, t: b * nt + t
    return pl.pallas_call(
        functools.partial(_dn_body, hg=hg, heads=heads),
        grid=(batch, ng, nt),
        in_specs=[
            pl.BlockSpec((BLK, wide), lambda b, g, t: (rows(b, g, t), g)),
            pl.BlockSpec((BLK, wide), lambda b, g, t: (rows(b, g, t), ng + g)),
            pl.BlockSpec((BLK, wide), lambda b, g, t: (rows(b, g, t), 2 * ng + g)),
            pl.BlockSpec((DN_CONV, wide), lambda b, g, t: (0, g)),
            pl.BlockSpec((DN_CONV, wide), lambda b, g, t: (0, ng + g)),
            pl.BlockSpec((DN_CONV, wide), lambda b, g, t: (0, 2 * ng + g)),
            pl.BlockSpec((BLK, HEAD_DIM), lambda b, g, t: (rows(b, g, t), 0)),
            pl.BlockSpec((BLK, wide), lambda b, g, t: (rows(b, g, t), g)),
            pl.BlockSpec((1, HEAD_DIM), lambda b, g, t: (0, 0)),
        ],
        out_specs=pl.BlockSpec((BLK, wide), lambda b, g, t: (rows(b, g, t), g)),
        out_shape=jax.ShapeDtypeStruct((m, d), BF16),
        scratch_shapes=[
            pltpu.VMEM((hg, HEAD_DIM, HEAD_DIM), F32),
            pltpu.VMEM((3, 8, wide), F32),
            pltpu.VMEM((HEAD_DIM, BLK), F32),
        ],
        compiler_params=_params(("parallel", "parallel", "arbitrary")),
        name="deltanet",
    )(dnqkv, dnqkv, dnqkv, conv_w, conv_w, conv_w, ba, z, ogain)


def _sb_tiles(qs, kjs, vjs, carry, suffix2, causal):
    zs = [_dot_nt(q, kj) for q, kj in zip(qs, kjs)]
    sps = [_softplus_fast(z.astype(BF16)) for z in zs]
    if causal is not None:
        sps = [jnp.where(causal, sp, jnp.zeros_like(sp)) for sp in sps]
    cums = [_dot(sp, suffix2) for sp in sps]
    ws = [jnp.exp(z - cum - jnp.concatenate([right, right], axis=1))
          for z, cum, (_, right) in zip(zs, cums, carry)]
    if causal is not None:
        ws = [jnp.where(causal, w, 0.0) for w in ws]
    return tuple((acc + _dot(w.astype(BF16), vj), right + jnp.broadcast_to(cum[:, 0:1], right.shape))
                 for w, vj, cum, (acc, right) in zip(ws, vjs, cums, carry))


def _sb_body(q_ref, k_ref, v_ref, o_ref, *, hs):
    qi = pl.program_id(2)
    row = lax.broadcasted_iota(jnp.int32, (BLK, BLK), 0)
    col = lax.broadcasted_iota(jnp.int32, (BLK, BLK), 1)
    suffix = jnp.where(row >= col, 1.0, 0.0).astype(BF16)
    suffix2 = suffix
    lanes = [slice(h * HEAD_DIM, (h + 1) * HEAD_DIM) for h in range(hs)]
    qs = [q_ref[:, sl] for sl in lanes]

    def sweep(start, causal, carry):
        kjs = [k_ref[pl.ds(start, BLK), sl] for sl in lanes]
        vjs = [v_ref[pl.ds(start, BLK), sl] for sl in lanes]
        return _sb_tiles(qs, kjs, vjs, carry, suffix2, causal)

    zeros = jnp.zeros((BLK, HEAD_DIM), F32)
    carry = sweep(pl.multiple_of(qi * BLK, BLK), col < row, tuple((zeros, zeros) for _ in lanes))

    def step(i, carry):
        return sweep(pl.multiple_of((qi - 1 - i) * BLK, BLK), None, carry)

    carry = lax.fori_loop(0, qi, step, carry)
    for h, sl in enumerate(lanes):
        o_ref[:, sl] = carry[h][0].astype(BF16)


def _stick_breaking(qb, kb, vb, *, batch, seq, heads, hs):
    m, d = qb.shape
    nq = seq // BLK
    wide = hs * HEAD_DIM
    return pl.pallas_call(
        functools.partial(_sb_body, hs=hs),
        grid=(batch, heads // hs, nq),
        in_specs=[
            pl.BlockSpec((BLK, wide), lambda b, h, i: (b * nq + i, h)),
            pl.BlockSpec((seq, wide), lambda b, h, i: (b, h)),
            pl.BlockSpec((seq, wide), lambda b, h, i: (b, h)),
        ],
        out_specs=pl.BlockSpec((BLK, wide), lambda b, h, i: (b * nq + i, h)),
        out_shape=jax.ShapeDtypeStruct((m, d), BF16),
        compiler_params=_params(("parallel", "parallel", "arbitrary")),
        name="stickbrk",
    )(qb, kb, vb)


def _merge_body(x_ref, oa_ref, ob_ref, ga_ref, gb_ref, wa_ref, wb_ref, wo_ref, o_ref):
    ya = _dot(oa_ref[...], wa_ref[...])
    yb = _dot(ob_ref[...], wb_ref[...])
    merged = _sigmoid(ga_ref[...]) * ya + _sigmoid(gb_ref[...]) * yb
    o_ref[...] = x_ref[...] + _dot(merged.astype(BF16), wo_ref[...])


def _merge(x, oa, ob, ga, gb, wa, wb, wo, *, tm):
    m, d = x.shape
    row = lambda i: (i, 0)
    tile = pl.BlockSpec((tm, d), row)
    return pl.pallas_call(
        _merge_body,
        grid=(m // tm,),
        in_specs=[tile, tile, tile, tile, tile, _resident((d, d)), _resident((d, d)), _resident((d, d))],
        out_specs=tile,
        out_shape=jax.ShapeDtypeStruct((m, d), F32),
        compiler_params=_params(("parallel",)),
        name="merge",
    )(x, oa, ob, ga, gb, wa, wb, wo)


def _row_tile(m, want):
    tm = min(want, m)
    assert m % tm == 0, (m, tm)
    return tm


def kernel(x, ffn1_norm, ffn1_w_in, ffn1_w_out, mix_norm, w_in, dn_conv_w, dn_a_log, dn_dt_bias,
           dn_out_norm, sb_q_norm, sb_k_norm, w_branch_a, w_branch_b, w_out,
           ffn2_norm, ffn2_w_in, ffn2_w_out):
    batch, seq, d = x.shape
    depth = ffn1_norm.shape[0]
    heads = d // HEAD_DIM
    assert seq % BLK == 0 and d % HEAD_DIM == 0 and 2 * heads <= HEAD_DIM
    m = batch * seq
    tm_ffn = _row_tile(m, 512)
    tm_proj = _row_tile(m, 256)
    tm_merge = _row_tile(m, 512)
    hg = 8 if heads % 8 == 0 else 1
    hs = 4 if heads % 4 == 0 else 1

    n_ba = 2 * heads
    w_main = jnp.concatenate([w_in[:, :, :4 * d], w_in[:, :, 4 * d + n_ba:]], axis=2).astype(BF16)
    w_ba = jnp.pad(w_in[:, :, 4 * d:4 * d + n_ba], ((0, 0), (0, 0), (0, HEAD_DIM - n_ba))).astype(BF16)
    lane_pad = lambda v: jnp.pad(v, ((0, 0), (heads, HEAD_DIM - n_ba)))[:, None, :]
    avec = lane_pad(-jnp.exp(dn_a_log.astype(F32)))
    dtvec = lane_pad(dn_dt_bias.astype(F32))
    f1_in, f1_out = ffn1_w_in.astype(BF16), ffn1_w_out.astype(BF16)
    f2_in, f2_out = ffn2_w_in.astype(BF16), ffn2_w_out.astype(BF16)
    wa, wb, wo = w_branch_a.astype(BF16), w_branch_b.astype(BF16), w_out.astype(BF16)

    xf = x.reshape(m, d)
    for l in range(depth):
        xf = _ffn(xf, ffn1_norm[l][None], f1_in[l], f1_out[l], tm=tm_ffn)
        dnqkv, z, qb, kb, vb, ga, gb, ba = _proj(
            xf, mix_norm[l][None], w_main[l], w_ba[l], sb_q_norm[l][None], sb_k_norm[l][None],
            avec[l], dtvec[l], tm=tm_proj, heads=heads)
        oa = _deltanet(dnqkv, dn_conv_w[l], ba, z, dn_out_norm[l][None],
                       batch=batch, seq=seq, heads=heads, hg=hg)
        ob = _stick_breaking(qb, kb, vb, batch=batch, seq=seq, heads=heads, hs=hs)
        xf = _merge(xf, oa, ob, ga, gb, wa[l], wb[l], wo[l], tm=tm_merge)
        xf = _ffn(xf, ffn2_norm[l][None], f2_in[l], f2_out[l], tm=tm_ffn)
    return xf.reshape(batch, seq, d)
```

```python
import functools

import jax
import jax.numpy as jnp
from jax import lax
from jax.experimental import pallas as pl
from jax.experimental.pallas import tpu as pltpu

F32 = jnp.float32
BF16 = jnp.bfloat16

HEAD_DIM = 128
DN_CONV = 4
INV_BASE = 8
RMS_EPS = 1e-6
L2_EPS = 1e-6
NEG_BIG = -1e30
LOG2E = 1.4426950408889634

MXU_DIM = 256
BLK = MXU_DIM
VMEM_LIMIT_BYTES = 56 * 1024 * 1024


def _dot(a, b):
    return jnp.dot(a, b, preferred_element_type=F32)


def _dot_nt(a, b):
    return lax.dot_general(a, b, (((1,), (1,)), ((), ())), preferred_element_type=F32)


def _dot_tn(a, b):
    return lax.dot_general(a, b, (((0,), (0,)), ((), ())), preferred_element_type=F32)


def _softplus(x):
    return jnp.maximum(x, 0.0) + jnp.log1p(jnp.exp(-jnp.abs(x)))


def _softplus_fast(x):
    one = jnp.ones((), x.dtype)
    return jnp.maximum(x, 0 * one) + jnp.log(one + jnp.exp2(jnp.abs(x) * (-LOG2E * one)))


def _sigmoid(x):
    return 1.0 / (1.0 + jnp.exp(-x))


def _rms_rows(x, gain):
    ms = jnp.mean(x * x, axis=-1, keepdims=True)
    return x * lax.rsqrt(ms + RMS_EPS) * gain


def _split3(x):
    p0 = x.astype(BF16)
    r1 = x - p0.astype(F32)
    p1 = r1.astype(BF16)
    p2 = (r1 - p1.astype(F32)).astype(BF16)
    return p0, p1, p2


def _resident(shape):
    return pl.BlockSpec(shape, lambda *_: (0,) * len(shape), pipeline_mode=pl.Buffered(1))


def _params(semantics):
    return pltpu.CompilerParams(dimension_semantics=semantics, vmem_limit_bytes=VMEM_LIMIT_BYTES)


def _ffn_body(x_ref, gain_ref, win_ref, wout_ref, o_ref, *, d_ff):
    x = x_ref[...]
    xn = _rms_rows(x, gain_ref[...]).astype(BF16)
    h = _dot(xn, win_ref[...])
    gate = h[:, :d_ff]
    up = h[:, d_ff:]
    act = (gate * _sigmoid(gate) * up).astype(BF16)
    o_ref[...] = x + 0.5 * _dot(act, wout_ref[...])


def _ffn(x, gain, w_in, w_out, *, tm):
    m, d = x.shape
    d_ff = w_out.shape[0]
    return pl.pallas_call(
        functools.partial(_ffn_body, d_ff=d_ff),
        grid=(m // tm,),
        in_specs=[
            pl.BlockSpec((tm, d), lambda i: (i, 0)),
            _resident((1, d)),
            _resident((d, 2 * d_ff)),
            _resident((d_ff, d)),
        ],
        out_specs=pl.BlockSpec((tm, d), lambda i: (i, 0)),
        out_shape=jax.ShapeDtypeStruct((m, d), F32),
        compiler_params=_params(("parallel",)),
        name="ffn",
    )(x, gain, w_in, w_out)


def _proj_body(x_ref, gain_ref, w_ref, wba_ref, qgain_ref, kgain_ref, avec_ref, dtvec_ref,
               dnqkv_ref, z_ref, qb_ref, kb_ref, vb_ref, ga_ref, gb_ref, ba_ref, *, d, heads):
    x = x_ref[...]
    xn = _rms_rows(x, gain_ref[...]).astype(BF16)
    dnqkv_ref[...] = _dot(xn, w_ref[:, 0:3 * d])
    z_ref[...] = _dot(xn, w_ref[:, 3 * d:4 * d])
    sq = _dot(xn, w_ref[:, 4 * d:5 * d])
    sk = _dot(xn, w_ref[:, 5 * d:6 * d])
    qgain = qgain_ref[...] * (HEAD_DIM ** -0.5)
    kgain = kgain_ref[...]
    for h in range(heads):
        sl = slice(h * HEAD_DIM, (h + 1) * HEAD_DIM)
        qb_ref[:, sl] = _rms_rows(sq[:, sl], qgain).astype(BF16)
        kb_ref[:, sl] = _rms_rows(sk[:, sl], kgain).astype(BF16)
    vb_ref[...] = _dot(xn, w_ref[:, 6 * d:7 * d]).astype(BF16)
    ga_ref[...] = _dot(xn, w_ref[:, 7 * d:8 * d])
    gb_ref[...] = _dot(xn, w_ref[:, 8 * d:9 * d])
    pba = _dot(xn, wba_ref[...])
    lane = lax.broadcasted_iota(jnp.int32, pba.shape, 1)
    beta = _sigmoid(pba)
    logdecay = avec_ref[...] * _softplus(pba + dtvec_ref[...])
    ba_ref[...] = jnp.where(lane < heads, beta, logdecay)


def _proj(x, gain, w, wba, qgain, kgain, avec, dtvec, *, tm, heads):
    m, d = x.shape
    row = lambda i: (i, 0)
    f32_out = lambda n: jax.ShapeDtypeStruct((m, n), F32)
    bf_out = lambda n: jax.ShapeDtypeStruct((m, n), BF16)
    return pl.pallas_call(
        functools.partial(_proj_body, d=d, heads=heads),
        grid=(m // tm,),
        in_specs=[
            pl.BlockSpec((tm, d), row),
            _resident((1, d)),
            _resident(w.shape),
            _resident(wba.shape),
            _resident((1, HEAD_DIM)),
            _resident((1, HEAD_DIM)),
            _resident((1, HEAD_DIM)),
            _resident((1, HEAD_DIM)),
        ],
        out_specs=[
            pl.BlockSpec((tm, 3 * d), row),
            pl.BlockSpec((tm, d), row),
            pl.BlockSpec((tm, d), row),
            pl.BlockSpec((tm, d), row),
            pl.BlockSpec((tm, d), row),
            pl.BlockSpec((tm, d), row),
            pl.BlockSpec((tm, d), row),
            pl.BlockSpec((tm, HEAD_DIM), row),
        ],
        out_shape=[f32_out(3 * d), f32_out(d), bf_out(d), bf_out(d), bf_out(d),
                   f32_out(d), f32_out(d), f32_out(HEAD_DIM)],
        compiler_params=_params(("parallel",)),
        name="proj",
    )(x, gain, w, wba, qgain, kgain, avec, dtvec)


def _shift_rows(x, tail, i):
    xs = pltpu.roll(x, i, 0)
    ts = pltpu.roll(tail, i, 0)
    row = lax.broadcasted_iota(jnp.int32, tail.shape, 0)
    head = jnp.where(row < i, ts, xs[:8])
    return jnp.concatenate([head, xs[8:]], axis=0)


def _conv_silu(x, tail, w):
    y = x * w[DN_CONV - 1:DN_CONV, :]
    for i in range(1, DN_CONV):
        y = y + _shift_rows(x, tail, i) * w[DN_CONV - 1 - i:DN_CONV - i, :]
    return y * _sigmoid(y)


def _column(tile, lane_idx):
    lane = lax.broadcasted_iota(jnp.int32, tile.shape, 1)
    return jnp.sum(jnp.where(lane == lane_idx, tile, 0.0), axis=1, keepdims=True)


def _inverse_masks(row, col):
    shift = INV_BASE.bit_length() - 1
    blk = lambda x, s: lax.shift_right_logical(x, s)
    base = blk(row, shift) == blk(col, shift)
    joins = []
    while (1 << shift) < BLK:
        joins.append((blk(row, shift + 1) == blk(col, shift + 1)) & (blk(row, shift) != blk(col, shift)))
        shift += 1
    return base, joins


def _unit_lower_inverse(lmats, eye, masks):
    base, joins = masks
    blocks = [jnp.where(base, l, 0.0) for l in lmats]
    invs = [eye - b for b in blocks]
    powers = [b.astype(BF16) for b in blocks]
    span = 2
    while span < INV_BASE:
        powers = [_dot(p, p).astype(BF16) for p in powers]
        invs = [t + _dot(t.astype(BF16), p) for t, p in zip(invs, powers)]
        span *= 2
    for join in joins:
        t16s = [t.astype(BF16) for t in invs]
        cts = [_dot(jnp.where(join, l, 0.0).astype(BF16), t16).astype(BF16) for l, t16 in zip(lmats, t16s)]
        invs = [t - _dot(t16, ct) for t, t16, ct in zip(invs, t16s, cts)]
    return invs


def _dn_body(xq_ref, xk_ref, xv_ref, wq_ref, wk_ref, wv_ref, ba_ref, z_ref, ogain_ref,
             o_ref, state_ref, tail_ref, gct_ref, *, hg, heads):
    t = pl.program_id(2)
    group = pl.program_id(1)

    @pl.when(t == 0)
    def _():
        state_ref[...] = jnp.zeros_like(state_ref)
        tail_ref[...] = jnp.zeros_like(tail_ref)

    xq = xq_ref[...]
    xk = xk_ref[...]
    xv = xv_ref[...]
    qa = _conv_silu(xq, tail_ref[0], wq_ref[...])
    ka = _conv_silu(xk, tail_ref[1], wk_ref[...])
    va = _conv_silu(xv, tail_ref[2], wv_ref[...])
    tail_ref[0] = xq[BLK - 8:]
    tail_ref[1] = xk[BLK - 8:]
    tail_ref[2] = xv[BLK - 8:]

    row = lax.broadcasted_iota(jnp.int32, (BLK, BLK), 0)
    col = lax.broadcasted_iota(jnp.int32, (BLK, BLK), 1)
    lower_incl = row >= col
    strict = row > col
    eye = jnp.where(row == col, 1.0, 0.0)
    masks = _inverse_masks(row, col)
    cum_mat = jnp.where(lower_incl, 1.0, 0.0).astype(BF16)

    ba = ba_ref[...]
    g0, g1, g2 = _split3(ba)
    gc_all = _dot(cum_mat, g0) + _dot(cum_mat, g1) + _dot(cum_mat, g2)
    gl_all = gc_all[BLK - 1:BLK, :]
    gct_ref[...] = gc_all.T
    egc_all = jnp.exp(gc_all)
    ekd_all = jnp.exp(gl_all - gc_all)
    egl_all = jnp.broadcast_to(jnp.exp(gl_all), (HEAD_DIM, HEAD_DIM))

    ogain = ogain_ref[...]
    js = range(hg)
    lanes = [slice(j * HEAD_DIM, (j + 1) * HEAD_DIM) for j in js]
    hs = [group * hg + j for j in js]
    l2 = lambda x: lax.rsqrt(jnp.sum(x * x, axis=-1, keepdims=True) + L2_EPS)
    qs = [qa[:, sl] * (l2(qa[:, sl]) * (HEAD_DIM ** -0.5)) for sl in lanes]
    ks = [ka[:, sl] * l2(ka[:, sl]) for sl in lanes]
    betas = [_column(ba, h) for h in hs]
    egcs = [_column(egc_all, heads + h) for h in hs]
    ekds = [_column(ekd_all, heads + h) for h in hs]
    egls = [_column(egl_all, heads + h) for h in hs]
    decays = [jnp.exp(jnp.where(lower_incl,
                                _column(gc_all, heads + h) - gct_ref[pl.ds(heads + h, 1), :], NEG_BIG))
              for h in hs]
    kbs = [k * b for k, b in zip(ks, betas)]
    k16s = [k.astype(BF16) for k in ks]
    lmats = [jnp.where(strict, _dot_nt(kb.astype(BF16), k16) * dec, 0.0)
             for kb, k16, dec in zip(kbs, k16s, decays)]
    invs = _unit_lower_inverse(lmats, eye, masks)
    rhss = [jnp.concatenate([va[:, sl] * b, kb * egc], axis=1).astype(BF16)
            for sl, b, kb, egc in zip(lanes, betas, kbs, egcs)]
    sols = [_dot(inv.astype(BF16), rhs) for inv, rhs in zip(invs, rhss)]
    attn16s = [(_dot_nt(q.astype(BF16), k16) * dec).astype(BF16) for q, k16, dec in zip(qs, k16s, decays)]
    qd16s = [(q * egc).astype(BF16) for q, egc in zip(qs, egcs)]
    kd16s = [(k * ekd).astype(BF16) for k, ekd in zip(ks, ekds)]
    states = [state_ref[j] for j in js]
    ws_qss = [_dot(jnp.concatenate([sol[:, HEAD_DIM:].astype(BF16), qd16], axis=0), s.astype(BF16))
              for sol, qd16, s in zip(sols, qd16s, states)]
    vnew16s = [(sol[:, :HEAD_DIM] - ws_qs[:BLK]).astype(BF16) for sol, ws_qs in zip(sols, ws_qss)]
    outs = [ws_qs[BLK:] + _dot(attn16, vnew16) for ws_qs, attn16, vnew16 in zip(ws_qss, attn16s, vnew16s)]
    for j, s, egl, kd16, vnew16 in zip(js, states, egls, kd16s, vnew16s):
        state_ref[j] = s * egl + _dot_tn(kd16, vnew16)
    for sl, o in zip(lanes, outs):
        zt = z_ref[:, sl]
        o_ref[:, sl] = (_rms_rows(o, ogain) * (zt * _sigmoid(zt))).astype(BF16)


def _deltanet(dnqkv, conv_w, ba, z, ogain, *, batch, seq, heads, hg):
    m = dnqkv.shape[0]
    d = heads * HEAD_DIM
    nt = seq // BLK
    ng = heads // hg
    wide = hg * HEAD_DIM
    rows = lambda b, g, t: b * nt + t
    return pl.pallas_call(
        functools.partial(_dn_body, hg=hg, heads=heads),
        grid=(batch, ng, nt),
        in_specs=[
            pl.BlockSpec((BLK, wide), lambda b, g, t: (rows(b, g, t), g)),
            pl.BlockSpec((BLK, wide), lambda b, g, t: (rows(b, g, t), ng + g)),
            pl.BlockSpec((BLK, wide), lambda b, g, t: (rows(b, g, t), 2 * ng + g)),
            pl.BlockSpec((DN_CONV, wide), lambda b, g, t: (0, g)),
            pl.BlockSpec((DN_CONV, wide), lambda b, g, t: (0, ng + g)),
            pl.BlockSpec((DN_CONV, wide), lambda b, g, t: (0, 2 * ng + g)),
            pl.BlockSpec((BLK, HEAD_DIM), lambda b, g, t: (rows(b, g, t), 0)),
            pl.BlockSpec((BLK, wide), lambda b, g, t: (rows(b, g, t), g)),
            pl.BlockSpec((1, HEAD_DIM), lambda b, g, t: (0, 0)),
        ],
        out_specs=pl.BlockSpec((BLK, wide), lambda b, g, t: (rows(b, g, t), g)),
        out_shape=jax.ShapeDtypeStruct((m, d), BF16),
        scratch_shapes=[
            pltpu.VMEM((hg, HEAD_DIM, HEAD_DIM), F32),
            pltpu.VMEM((3, 8, wide), F32),
            pltpu.VMEM((HEAD_DIM, BLK), F32),
        ],
        compiler_params=_params(("parallel", "parallel", "arbitrary")),
        name="deltanet",
    )(dnqkv, dnqkv, dnqkv, conv_w, conv_w, conv_w, ba, z, ogain)


def _sb_body(qa_ref, qb_ref, k_ref, v_ref, o_ref, qs_ref, acc_ref, right_ref, *, hs, nq):
    p = pl.program_id(2)
    long_blk = nq - 1 - p
    row = lax.broadcasted_iota(jnp.int32, (BLK, BLK), 0)
    col = lax.broadcasted_iota(jnp.int32, (BLK, BLK), 1)
    suffix = jnp.where(row >= col, 1.0, 0.0).astype(BF16)
    causal = col < row
    lanes = [slice(h * HEAD_DIM, (h + 1) * HEAD_DIM) for h in range(hs)]
    qs_ref[0] = qa_ref[...]
    qs_ref[1] = qb_ref[...]

    tiles = [(0, long_blk, True), (1, p, True)]
    for m in range(nq - 1):
        in_short = m >= long_blk
        tiles.append((jnp.where(in_short, 1, 0), jnp.where(in_short, p - 1 - (m - long_blk), long_blk - 1 - m), False))

    def scores(slot, kblk):
        start = pl.multiple_of(kblk * BLK, BLK)
        return [_dot_nt(qs_ref[slot, :, sl], k_ref[pl.ds(start, BLK), sl]) for sl in lanes]

    def cumsums(zs, diagonal):
        sps = [_softplus_fast(z.astype(BF16)) for z in zs]
        if diagonal:
            sps = [jnp.where(causal, sp, jnp.zeros_like(sp)) for sp in sps]
        return [_dot(sp, suffix) for sp in sps]

    def accumulate(slot, kblk, diagonal, zs, cums):
        start = pl.multiple_of(kblk * BLK, BLK)
        if diagonal:
            ws = [jnp.where(causal, jnp.exp(z - cum), 0.0) for z, cum in zip(zs, cums)]
        else:
            rights = [right_ref[slot, h] for h in range(hs)]
            ws = [jnp.exp(z - cum - jnp.concatenate([r, r], axis=1)) for z, cum, r in zip(zs, cums, rights)]
        pvs = [_dot(w.astype(BF16), v_ref[pl.ds(start, BLK), sl]) for w, sl in zip(ws, lanes)]
        totals = [jnp.broadcast_to(cum[:, 0:1], (BLK, HEAD_DIM)) for cum in cums]
        for h in range(hs):
            if diagonal:
                acc_ref[slot, h] = pvs[h]
                right_ref[slot, h] = totals[h]
            else:
                acc_ref[slot, h] = acc_ref[slot, h] + pvs[h]
                right_ref[slot, h] = rights[h] + totals[h]

    n_tiles = len(tiles)
    zs = {0: scores(*tiles[0][:2]), 1: scores(*tiles[1][:2])}
    cums = {0: cumsums(zs[0], tiles[0][2])}
    for n in range(n_tiles):
        if n + 2 < n_tiles:
            zs[n + 2] = scores(*tiles[n + 2][:2])
        if n + 1 < n_tiles:
            cums[n + 1] = cumsums(zs[n + 1], tiles[n + 1][2])
        accumulate(*tiles[n], zs.pop(n), cums.pop(n))

    for slot in range(2):
        for h, sl in enumerate(lanes):
            o_ref[slot, :, sl] = acc_ref[slot, h].astype(BF16)


def _stick_breaking(qb, kb, vb, *, batch, seq, heads, hs):
    m, d = qb.shape
    nq = seq // BLK
    assert nq % 2 == 0
    wide = hs * HEAD_DIM
    return pl.pallas_call(
        functools.partial(_sb_body, hs=hs, nq=nq),
        grid=(batch, heads // hs, nq // 2),
        in_specs=[
            pl.BlockSpec((BLK, wide), lambda b, h, p: (b * nq + (nq - 1 - p), h)),
            pl.BlockSpec((BLK, wide), lambda b, h, p: (b * nq + p, h)),
            pl.BlockSpec((seq, wide), lambda b, h, p: (b, h)),
            pl.BlockSpec((seq, wide), lambda b, h, p: (b, h)),
        ],
        out_specs=pl.BlockSpec((None, None, 2, BLK, wide), lambda b, h, p: (b, p, 0, 0, h)),
        out_shape=jax.ShapeDtypeStruct((batch, nq // 2, 2, BLK, d), BF16),
        scratch_shapes=[
            pltpu.VMEM((2, BLK, wide), BF16),
            pltpu.VMEM((2, hs, BLK, HEAD_DIM), F32),
            pltpu.VMEM((2, hs, BLK, HEAD_DIM), F32),
        ],
        compiler_params=_params(("parallel", "parallel", "arbitrary")),
        name="stickbrk",
    )(qb, qb, kb, vb)


def _merge_body(x_ref, oa_ref, ob_ref, ga_ref, gb_ref, wa_ref, wb_ref, wo_ref, o_ref):
    ya = _dot(oa_ref[...], wa_ref[...])
    yb = _dot(ob_ref[...], wb_ref[...])
    merged = _sigmoid(ga_ref[...]) * ya + _sigmoid(gb_ref[...]) * yb
    o_ref[...] = x_ref[...] + _dot(merged.astype(BF16), wo_ref[...])


def _merge(x, oa, ob, ga, gb, wa, wb, wo, *, nq):
    m, d = x.shape
    tile = pl.BlockSpec((BLK, d), lambda i: (i, 0))

    def ob_block(i):
        blk = lax.rem(i, nq)
        return (lax.div(i, nq), jnp.minimum(blk, nq - 1 - blk), jnp.where(blk < nq // 2, 1, 0), 0, 0)

    return pl.pallas_call(
        _merge_body,
        grid=(m // BLK,),
        in_specs=[tile, tile, pl.BlockSpec((None, None, None, BLK, d), ob_block), tile, tile,
                  _resident((d, d)), _resident((d, d)), _resident((d, d))],
        out_specs=tile,
        out_shape=jax.ShapeDtypeStruct((m, d), F32),
        compiler_params=_params(("parallel",)),
        name="merge",
    )(x, oa, ob, ga, gb, wa, wb, wo)


def _row_tile(m, want):
    tm = min(want, m)
    assert m % tm == 0, (m, tm)
    return tm


def kernel(x, ffn1_norm, ffn1_w_in, ffn1_w_out, mix_norm, w_in, dn_conv_w, dn_a_log, dn_dt_bias,
           dn_out_norm, sb_q_norm, sb_k_norm, w_branch_a, w_branch_b, w_out,
           ffn2_norm, ffn2_w_in, ffn2_w_out):
    batch, seq, d = x.shape
    depth = ffn1_norm.shape[0]
    heads = d // HEAD_DIM
    assert seq % (2 * BLK) == 0 and d % HEAD_DIM == 0 and 2 * heads <= HEAD_DIM
    m = batch * seq
    tm_ffn = _row_tile(m, 512)
    tm_proj = _row_tile(m, 256)
    hg = 8 if heads % 8 == 0 else 1
    hs = 4 if heads % 4 == 0 else 1

    n_ba = 2 * heads
    w_main = jnp.concatenate([w_in[:, :, :4 * d], w_in[:, :, 4 * d + n_ba:]], axis=2).astype(BF16)
    w_ba = jnp.pad(w_in[:, :, 4 * d:4 * d + n_ba], ((0, 0), (0, 0), (0, HEAD_DIM - n_ba))).astype(BF16)
    lane_pad = lambda v: jnp.pad(v, ((0, 0), (heads, HEAD_DIM - n_ba)))[:, None, :]
    avec = lane_pad(-jnp.exp(dn_a_log.astype(F32)))
    dtvec = lane_pad(dn_dt_bias.astype(F32))
    f1_in, f1_out = ffn1_w_in.astype(BF16), ffn1_w_out.astype(BF16)
    f2_in, f2_out = ffn2_w_in.astype(BF16), ffn2_w_out.astype(BF16)
    wa, wb, wo = w_branch_a.astype(BF16), w_branch_b.astype(BF16), w_out.astype(BF16)

    xf = x.reshape(m, d)
    for l in range(depth):
        xf = _ffn(xf, ffn1_norm[l][None], f1_in[l], f1_out[l], tm=tm_ffn)
        dnqkv, z, qb, kb, vb, ga, gb, ba = _proj(
            xf, mix_norm[l][None], w_main[l], w_ba[l], sb_q_norm[l][None], sb_k_norm[l][None],
            avec[l], dtvec[l], tm=tm_proj, heads=heads)
        oa = _deltanet(dnqkv, dn_conv_w[l], ba, z, dn_out_norm[l][None],
                       batch=batch, seq=seq, heads=heads, hg=hg)
        ob = _stick_breaking(qb, kb, vb, batch=batch, seq=seq, heads=heads, hs=hs)
        xf = _merge(xf, oa, ob, ga, gb, wa[l], wb[l], wo[l], nq=seq // BLK)
        xf = _ffn(xf, ffn2_norm[l][None], f2_in[l], f2_out[l], tm=tm_ffn)
    return xf.reshape(batch, seq, d)
```

```python
import functools

import jax
import jax.numpy as jnp
from jax import lax
from jax.experimental import pallas as pl
from jax.experimental.pallas import tpu as pltpu

F32 = jnp.float32
BF16 = jnp.bfloat16

HEAD_DIM = 128
DN_CONV = 4
INV_BASE = 8
RMS_EPS = 1e-6
L2_EPS = 1e-6
NEG_BIG = -1e30
LOG2E = 1.4426950408889634

MXU_DIM = 256
BLK = MXU_DIM
VMEM_LIMIT_BYTES = 56 * 1024 * 1024


def _dot(a, b):
    return jnp.dot(a, b, preferred_element_type=F32)


def _dot_nt(a, b):
    return lax.dot_general(a, b, (((1,), (1,)), ((), ())), preferred_element_type=F32)


def _dot_tn(a, b):
    return lax.dot_general(a, b, (((0,), (0,)), ((), ())), preferred_element_type=F32)


def _softplus(x):
    return jnp.maximum(x, 0.0) + jnp.log1p(jnp.exp(-jnp.abs(x)))


def _softplus_fast(x):
    one = jnp.ones((), x.dtype)
    return jnp.maximum(x, 0 * one) + jnp.log(one + jnp.exp2(jnp.abs(x) * (-LOG2E * one)))


def _sigmoid(x):
    return 1.0 / (1.0 + jnp.exp(-x))


def _rms_rows(x, gain):
    ms = jnp.mean(x * x, axis=-1, keepdims=True)
    return x * lax.rsqrt(ms + RMS_EPS) * gain


def _split3(x):
    p0 = x.astype(BF16)
    r1 = x - p0.astype(F32)
    p1 = r1.astype(BF16)
    p2 = (r1 - p1.astype(F32)).astype(BF16)
    return p0, p1, p2


def _resident(shape):
    return pl.BlockSpec(shape, lambda *_: (0,) * len(shape), pipeline_mode=pl.Buffered(1))


def _resident_layer(stacked, layer):
    rest = stacked.shape[1:]
    return pl.BlockSpec((None,) + rest, lambda *_: (layer,) + (0,) * len(rest), pipeline_mode=pl.Buffered(1))


def _params(semantics):
    return pltpu.CompilerParams(dimension_semantics=semantics, vmem_limit_bytes=VMEM_LIMIT_BYTES)


def _ffn_body(x_ref, gain_ref, win_ref, wout_ref, o_ref, *, d_ff):
    x = x_ref[...]
    xn = _rms_rows(x, gain_ref[...]).astype(BF16)
    h = _dot(xn, win_ref[...])
    gate = h[:, :d_ff]
    up = h[:, d_ff:]
    act = (gate * _sigmoid(gate) * up).astype(BF16)
    o_ref[...] = x + 0.5 * _dot(act, wout_ref[...])


def _ffn(x, gain, w_in, w_out, layer, *, tm):
    m, d = x.shape
    d_ff = w_out.shape[1]
    return pl.pallas_call(
        functools.partial(_ffn_body, d_ff=d_ff),
        grid=(m // tm,),
        in_specs=[
            pl.BlockSpec((tm, d), lambda i: (i, 0)),
            _resident((1, d)),
            _resident_layer(w_in, layer),
            _resident_layer(w_out, layer),
        ],
        out_specs=pl.BlockSpec((tm, d), lambda i: (i, 0)),
        out_shape=jax.ShapeDtypeStruct((m, d), F32),
        compiler_params=_params(("parallel",)),
        name="ffn",
    )(x, gain, w_in, w_out)


def _proj_body(x_ref, gain_ref, w_ref, wba_ref, qgain_ref, kgain_ref, avec_ref, dtvec_ref,
               dnqkv_ref, z_ref, qb_ref, kb_ref, vb_ref, ga_ref, gb_ref, ba_ref, *, d, heads):
    x = x_ref[...]
    xn = _rms_rows(x, gain_ref[...]).astype(BF16)
    dnqkv_ref[...] = _dot(xn, w_ref[:, 0:3 * d])
    z_ref[...] = _dot(xn, w_ref[:, 3 * d:4 * d])
    sq = _dot(xn, w_ref[:, 4 * d:5 * d])
    sk = _dot(xn, w_ref[:, 5 * d:6 * d])
    qgain = qgain_ref[...] * (HEAD_DIM ** -0.5)
    kgain = kgain_ref[...]
    for h in range(heads):
        sl = slice(h * HEAD_DIM, (h + 1) * HEAD_DIM)
        qb_ref[:, sl] = _rms_rows(sq[:, sl], qgain).astype(BF16)
        kb_ref[:, sl] = _rms_rows(sk[:, sl], kgain).astype(BF16)
    vb_ref[...] = _dot(xn, w_ref[:, 6 * d:7 * d]).astype(BF16)
    ga_ref[...] = _dot(xn, w_ref[:, 7 * d:8 * d])
    gb_ref[...] = _dot(xn, w_ref[:, 8 * d:9 * d])
    pba = _dot(xn, wba_ref[...])
    lane = lax.broadcasted_iota(jnp.int32, pba.shape, 1)
    beta = _sigmoid(pba)
    logdecay = avec_ref[...] * _softplus(pba + dtvec_ref[...])
    ba_ref[...] = jnp.where(lane < heads, beta, logdecay)


def _proj(x, gain, w, wba, layer, qgain, kgain, avec, dtvec, *, tm, heads):
    m, d = x.shape
    row = lambda i: (i, 0)
    f32_out = lambda n: jax.ShapeDtypeStruct((m, n), F32)
    bf_out = lambda n: jax.ShapeDtypeStruct((m, n), BF16)
    return pl.pallas_call(
        functools.partial(_proj_body, d=d, heads=heads),
        grid=(m // tm,),
        in_specs=[
            pl.BlockSpec((tm, d), row),
            _resident((1, d)),
            _resident_layer(w, layer),
            _resident_layer(wba, layer),
            _resident((1, HEAD_DIM)),
            _resident((1, HEAD_DIM)),
            _resident((1, HEAD_DIM)),
            _resident((1, HEAD_DIM)),
        ],
        out_specs=[
            pl.BlockSpec((tm, 3 * d), row),
            pl.BlockSpec((tm, d), row),
            pl.BlockSpec((tm, d), row),
            pl.BlockSpec((tm, d), row),
            pl.BlockSpec((tm, d), row),
            pl.BlockSpec((tm, d), row),
            pl.BlockSpec((tm, d), row),
            pl.BlockSpec((tm, HEAD_DIM), row),
        ],
        out_shape=[f32_out(3 * d), f32_out(d), bf_out(d), bf_out(d), bf_out(d),
                   f32_out(d), f32_out(d), f32_out(HEAD_DIM)],
        compiler_params=_params(("parallel",)),
        name="proj",
    )(x, gain, w, wba, qgain, kgain, avec, dtvec)


def _conv_silu(x_ref, w_ref, pre_ref, act_ref, idx, sl):
    rows = BLK // DN_CONV
    pre_ref[idx, 8:, :] = x_ref[:, sl]
    w = w_ref[:, sl]
    for r in range(DN_CONV):
        y = None
        for i in range(DN_CONV):
            window = pre_ref[idx, pl.ds(8 + r - i, rows, stride=DN_CONV), :]
            term = window * w[DN_CONV - 1 - i:DN_CONV - i, :]
            y = term if y is None else y + term
        act_ref[idx, pl.ds(r, rows, stride=DN_CONV), :] = y * _sigmoid(y)
    pre_ref[idx, 0:8, :] = x_ref[BLK - 8:, sl]


def _column(tile, lane_idx):
    lane = lax.broadcasted_iota(jnp.int32, tile.shape, 1)
    return jnp.sum(jnp.where(lane == lane_idx, tile, 0.0), axis=1, keepdims=True)


def _inverse_masks(row, col):
    shift = INV_BASE.bit_length() - 1
    blk = lambda x, s: lax.shift_right_logical(x, s)
    base = blk(row, shift) == blk(col, shift)
    joins = []
    while (1 << shift) < BLK:
        joins.append((blk(row, shift + 1) == blk(col, shift + 1)) & (blk(row, shift) != blk(col, shift)))
        shift += 1
    return base, joins


def _unit_lower_inverse(lmats, eye, masks):
    base, joins = masks
    blocks = [jnp.where(base, l, 0.0) for l in lmats]
    invs = [eye - b for b in blocks]
    powers = [b.astype(BF16) for b in blocks]
    span = 2
    while span < INV_BASE:
        powers = [_dot(p, p).astype(BF16) for p in powers]
        invs = [t + _dot(t.astype(BF16), p) for t, p in zip(invs, powers)]
        span *= 2
    for join in joins:
        t16s = [t.astype(BF16) for t in invs]
        cts = [_dot(jnp.where(join, l, 0.0).astype(BF16), t16).astype(BF16) for l, t16 in zip(lmats, t16s)]
        invs = [t - _dot(t16, ct) for t, t16, ct in zip(invs, t16s, cts)]
    return invs


def _dn_body(xq_ref, xk_ref, xv_ref, wq_ref, wk_ref, wv_ref, ba_ref, z_ref, ogain_ref,
             o_ref, state_ref, pre_ref, act_ref, gct_ref, *, hg, heads):
    t = pl.program_id(2)
    group = pl.program_id(1)
    js = range(hg)
    lanes = [slice(j * HEAD_DIM, (j + 1) * HEAD_DIM) for j in js]

    @pl.when(t == 0)
    def _():
        state_ref[...] = jnp.zeros_like(state_ref)
        pre_ref[:, 0:8, :] = jnp.zeros((3 * hg, 8, HEAD_DIM), F32)

    for part, (x_ref, w_ref) in enumerate(((xq_ref, wq_ref), (xk_ref, wk_ref), (xv_ref, wv_ref))):
        for j in js:
            _conv_silu(x_ref, w_ref, pre_ref, act_ref, part * hg + j, lanes[j])

    row = lax.broadcasted_iota(jnp.int32, (BLK, BLK), 0)
    col = lax.broadcasted_iota(jnp.int32, (BLK, BLK), 1)
    lower_incl = row >= col
    strict = row > col
    eye = jnp.where(row == col, 1.0, 0.0)
    masks = _inverse_masks(row, col)
    cum_mat = jnp.where(lower_incl, 1.0, 0.0).astype(BF16)

    ba = ba_ref[...]
    g0, g1, g2 = _split3(ba)
    gc_all = _dot(cum_mat, g0) + _dot(cum_mat, g1) + _dot(cum_mat, g2)
    gl_all = gc_all[BLK - 1:BLK, :]
    gct_ref[...] = gc_all.T
    egc_all = jnp.exp(gc_all)
    ekd_all = jnp.exp(gl_all - gc_all)
    egl_all = jnp.broadcast_to(jnp.exp(gl_all), (HEAD_DIM, HEAD_DIM))

    ogain = ogain_ref[...]
    hs = [group * hg + j for j in js]
    l2 = lambda x: lax.rsqrt(jnp.sum(x * x, axis=-1, keepdims=True) + L2_EPS)
    qs = [act_ref[j] * (l2(act_ref[j]) * (HEAD_DIM ** -0.5)) for j in js]
    ks = [act_ref[hg + j] * l2(act_ref[hg + j]) for j in js]
    vs = [act_ref[2 * hg + j] for j in js]
    betas = [_column(ba, h) for h in hs]
    egcs = [_column(egc_all, heads + h) for h in hs]
    ekds = [_column(ekd_all, heads + h) for h in hs]
    egls = [_column(egl_all, heads + h) for h in hs]
    decays = [jnp.exp(jnp.where(lower_incl,
                                _column(gc_all, heads + h) - gct_ref[pl.ds(heads + h, 1), :], NEG_BIG))
              for h in hs]
    kbs = [k * b for k, b in zip(ks, betas)]
    k16s = [k.astype(BF16) for k in ks]
    lmats = [jnp.where(strict, _dot_nt(kb.astype(BF16), k16) * dec, 0.0)
             for kb, k16, dec in zip(kbs, k16s, decays)]
    invs = _unit_lower_inverse(lmats, eye, masks)
    rhss = [jnp.concatenate([v * b, kb * egc], axis=1).astype(BF16)
            for v, b, kb, egc in zip(vs, betas, kbs, egcs)]
    sols = [_dot(inv.astype(BF16), rhs) for inv, rhs in zip(invs, rhss)]
    attn16s = [(_dot_nt(q.astype(BF16), k16) * dec).astype(BF16) for q, k16, dec in zip(qs, k16s, decays)]
    qd16s = [(q * egc).astype(BF16) for q, egc in zip(qs, egcs)]
    kd16s = [(k * ekd).astype(BF16) for k, ekd in zip(ks, ekds)]
    states = [state_ref[j] for j in js]
    ws_qss = [_dot(jnp.concatenate([sol[:, HEAD_DIM:].astype(BF16), qd16], axis=0), s.astype(BF16))
              for sol, qd16, s in zip(sols, qd16s, states)]
    vnew16s = [(sol[:, :HEAD_DIM] - ws_qs[:BLK]).astype(BF16) for sol, ws_qs in zip(sols, ws_qss)]
    outs = [ws_qs[BLK:] + _dot(attn16, vnew16) for ws_qs, attn16, vnew16 in zip(ws_qss, attn16s, vnew16s)]
    for j, s, egl, kd16, vnew16 in zip(js, states, egls, kd16s, vnew16s):
        state_ref[j] = s * egl + _dot_tn(kd16, vnew16)
    for sl, o in zip(lanes, outs):
        zt = z_ref[:, sl]
        o_ref[:, sl] = (_rms_rows(o, ogain) * (zt * _sigmoid(zt))).astype(BF16)


def _deltanet(dnqkv, conv_w, layer, ba, z, ogain, *, batch, seq, heads, hg):
    m = dnqkv.shape[0]
    d = heads * HEAD_DIM
    nt = seq // BLK
    ng = heads // hg
    wide = hg * HEAD_DIM
    rows = lambda b, g, t: b * nt + t
    x_part = lambda part: pl.BlockSpec((BLK, wide), lambda b, g, t: (rows(b, g, t), part * ng + g))
    w_part = lambda part: pl.BlockSpec((None, DN_CONV, wide), lambda b, g, t: (layer, 0, part * ng + g))
    return pl.pallas_call(
        functools.partial(_dn_body, hg=hg, heads=heads),
        grid=(batch, ng, nt),
        in_specs=[
            x_part(0), x_part(1), x_part(2),
            w_part(0), w_part(1), w_part(2),
            pl.BlockSpec((BLK, HEAD_DIM), lambda b, g, t: (rows(b, g, t), 0)),
            pl.BlockSpec((BLK, wide), lambda b, g, t: (rows(b, g, t), g)),
            pl.BlockSpec((1, HEAD_DIM), lambda b, g, t: (0, 0)),
        ],
        out_specs=pl.BlockSpec((BLK, wide), lambda b, g, t: (rows(b, g, t), g)),
        out_shape=jax.ShapeDtypeStruct((m, d), BF16),
        scratch_shapes=[
            pltpu.VMEM((hg, HEAD_DIM, HEAD_DIM), F32),
            pltpu.VMEM((3 * hg, 8 + BLK, HEAD_DIM), F32),
            pltpu.VMEM((3 * hg, BLK, HEAD_DIM), F32),
            pltpu.VMEM((HEAD_DIM, BLK), F32),
        ],
        compiler_params=_params(("parallel", "parallel", "arbitrary")),
        name="deltanet",
    )(dnqkv, dnqkv, dnqkv, conv_w, conv_w, conv_w, ba, z, ogain)


def _sb_body(qa_ref, qb_ref, k_ref, v_ref, o_ref, qs_ref, acc_ref, right_ref, *, hs, nq):
    p = pl.program_id(2)
    long_blk = nq - 1 - p
    row = lax.broadcasted_iota(jnp.int32, (BLK, BLK), 0)
    col = lax.broadcasted_iota(jnp.int32, (BLK, BLK), 1)
    suffix = jnp.where(row >= col, 1.0, 0.0).astype(BF16)
    causal = col < row
    lanes = [slice(h * HEAD_DIM, (h + 1) * HEAD_DIM) for h in range(hs)]
    qs_ref[0] = qa_ref[...]
    qs_ref[1] = qb_ref[...]

    tiles = [(0, long_blk, True), (1, p, True)]
    for m in range(nq - 1):
        in_short = m >= long_blk
        tiles.append((jnp.where(in_short, 1, 0), jnp.where(in_short, p - 1 - (m - long_blk), long_blk - 1 - m), False))

    def scores(slot, kblk):
        start = pl.multiple_of(kblk * BLK, BLK)
        return [_dot_nt(qs_ref[slot, :, sl], k_ref[pl.ds(start, BLK), sl]) for sl in lanes]

    def cumsums(zs, diagonal):
        sps = [_softplus_fast(z.astype(BF16)) for z in zs]
        if diagonal:
            sps = [jnp.where(causal, sp, jnp.zeros_like(sp)) for sp in sps]
        return [_dot(sp, suffix) for sp in sps]

    def accumulate(slot, kblk, diagonal, zs, cums):
        start = pl.multiple_of(kblk * BLK, BLK)
        if diagonal:
            ws = [jnp.where(causal, jnp.exp(z - cum), 0.0) for z, cum in zip(zs, cums)]
        else:
            rights = [right_ref[slot, h] for h in range(hs)]
            ws = [jnp.exp(z - cum - jnp.concatenate([r, r], axis=1)) for z, cum, r in zip(zs, cums, rights)]
        pvs = [_dot(w.astype(BF16), v_ref[pl.ds(start, BLK), sl]) for w, sl in zip(ws, lanes)]
        totals = [jnp.broadcast_to(cum[:, 0:1], (BLK, HEAD_DIM)) for cum in cums]
        for h in range(hs):
            if diagonal:
                acc_ref[slot, h] = pvs[h]
                right_ref[slot, h] = totals[h]
            else:
                acc_ref[slot, h] = acc_ref[slot, h] + pvs[h]
                right_ref[slot, h] = rights[h] + totals[h]

    n_tiles = len(tiles)
    zs = {0: scores(*tiles[0][:2]), 1: scores(*tiles[1][:2])}
    cums = {0: cumsums(zs[0], tiles[0][2])}
    for n in range(n_tiles):
        if n + 2 < n_tiles:
            zs[n + 2] = scores(*tiles[n + 2][:2])
        if n + 1 < n_tiles:
            cums[n + 1] = cumsums(zs[n + 1], tiles[n + 1][2])
        accumulate(*tiles[n], zs.pop(n), cums.pop(n))

    for slot in range(2):
        for h, sl in enumerate(lanes):
            o_ref[slot, :, sl] = acc_ref[slot, h].astype(BF16)


def _stick_breaking(qb, kb, vb, *, batch, seq, heads, hs):
    m, d = qb.shape
    nq = seq // BLK
    assert nq % 2 == 0
    wide = hs * HEAD_DIM
    return pl.pallas_call(
        functools.partial(_sb_body, hs=hs, nq=nq),
        grid=(batch, heads // hs, nq // 2),
        in_specs=[
            pl.BlockSpec((BLK, wide), lambda b, h, p: (b * nq + (nq - 1 - p), h)),
            pl.BlockSpec((BLK, wide), lambda b, h, p: (b * nq + p, h)),
            pl.BlockSpec((seq, wide), lambda b, h, p: (b, h)),
            pl.BlockSpec((seq, wide), lambda b, h, p: (b, h)),
        ],
        out_specs=pl.BlockSpec((None, None, 2, BLK, wide), lambda b, h, p: (b, p, 0, 0, h)),
        out_shape=jax.ShapeDtypeStruct((batch, nq // 2, 2, BLK, d), BF16),
        scratch_shapes=[
            pltpu.VMEM((2, BLK, wide), BF16),
            pltpu.VMEM((2, hs, BLK, HEAD_DIM), F32),
            pltpu.VMEM((2, hs, BLK, HEAD_DIM), F32),
        ],
        compiler_params=_params(("parallel", "parallel", "arbitrary")),
        name="stickbrk",
    )(qb, qb, kb, vb)


def _merge_body(x_ref, oa_ref, ob_ref, ga_ref, gb_ref, wa_ref, wb_ref, wo_ref, o_ref):
    ya = _dot(oa_ref[...], wa_ref[...])
    yb = _dot(ob_ref[...], wb_ref[...])
    merged = _sigmoid(ga_ref[...]) * ya + _sigmoid(gb_ref[...]) * yb
    o_ref[...] = x_ref[...] + _dot(merged.astype(BF16), wo_ref[...])


def _merge(x, oa, ob, ga, gb, wa, wb, wo, layer, *, nq):
    m, d = x.shape
    tile = pl.BlockSpec((BLK, d), lambda i: (i, 0))

    def ob_block(i):
        blk = lax.rem(i, nq)
        return (lax.div(i, nq), jnp.minimum(blk, nq - 1 - blk), jnp.where(blk < nq // 2, 1, 0), 0, 0)

    return pl.pallas_call(
        _merge_body,
        grid=(m // BLK,),
        in_specs=[tile, tile, pl.BlockSpec((None, None, None, BLK, d), ob_block), tile, tile,
                  _resident_layer(wa, layer), _resident_layer(wb, layer), _resident_layer(wo, layer)],
        out_specs=tile,
        out_shape=jax.ShapeDtypeStruct((m, d), F32),
        compiler_params=_params(("parallel",)),
        name="merge",
    )(x, oa, ob, ga, gb, wa, wb, wo)


def _row_tile(m, want):
    tm = min(want, m)
    assert m % tm == 0, (m, tm)
    return tm


def kernel(x, ffn1_norm, ffn1_w_in, ffn1_w_out, mix_norm, w_in, dn_conv_w, dn_a_log, dn_dt_bias,
           dn_out_norm, sb_q_norm, sb_k_norm, w_branch_a, w_branch_b, w_out,
           ffn2_norm, ffn2_w_in, ffn2_w_out):
    batch, seq, d = x.shape
    depth = ffn1_norm.shape[0]
    heads = d // HEAD_DIM
    assert seq % (2 * BLK) == 0 and d % HEAD_DIM == 0 and 2 * heads <= HEAD_DIM
    m = batch * seq
    tm_ffn = _row_tile(m, 512)
    tm_proj = _row_tile(m, 256)
    hg = 8 if heads % 8 == 0 else 1
    hs = 4 if heads % 4 == 0 else 1

    n_ba = 2 * heads
    w_main = jnp.concatenate([w_in[:, :, :4 * d], w_in[:, :, 4 * d + n_ba:]], axis=2).astype(BF16)
    w_ba = jnp.pad(w_in[:, :, 4 * d:4 * d + n_ba], ((0, 0), (0, 0), (0, HEAD_DIM - n_ba))).astype(BF16)
    lane_pad = lambda v: jnp.pad(v, ((0, 0), (heads, HEAD_DIM - n_ba)))[:, None, :]
    avec = lane_pad(-jnp.exp(dn_a_log.astype(F32)))
    dtvec = lane_pad(dn_dt_bias.astype(F32))
    f1_in, f1_out = ffn1_w_in.astype(BF16), ffn1_w_out.astype(BF16)
    f2_in, f2_out = ffn2_w_in.astype(BF16), ffn2_w_out.astype(BF16)
    wa, wb, wo = w_branch_a.astype(BF16), w_branch_b.astype(BF16), w_out.astype(BF16)

    xf = x.reshape(m, d)
    for l in range(depth):
        xf = _ffn(xf, ffn1_norm[l][None], f1_in, f1_out, l, tm=tm_ffn)
        dnqkv, z, qb, kb, vb, ga, gb, ba = _proj(
            xf, mix_norm[l][None], w_main, w_ba, l, sb_q_norm[l][None], sb_k_norm[l][None],
            avec[l], dtvec[l], tm=tm_proj, heads=heads)
        oa = _deltanet(dnqkv, dn_conv_w, l, ba, z, dn_out_norm[l][None],
                       batch=batch, seq=seq, heads=heads, hg=hg)
        ob = _stick_breaking(qb, kb, vb, batch=batch, seq=seq, heads=heads, hs=hs)
        xf = _merge(xf, oa, ob, ga, gb, wa, wb, wo, l, nq=seq // BLK)
        xf = _ffn(xf, ffn2_norm[l][None], f2_in, f2_out, l, tm=tm_ffn)
    return xf.reshape(batch, seq, d)
```

```python
import functools

import jax
import jax.numpy as jnp
from jax import lax
from jax.experimental import pallas as pl
from jax.experimental.pallas import tpu as pltpu

F32 = jnp.float32
BF16 = jnp.bfloat16

HEAD_DIM = 128
DN_CONV = 4
INV_BASE = 8
RMS_EPS = 1e-6
L2_EPS = 1e-6
NEG_BIG = -1e30
LOG2E = 1.4426950408889634

MXU_DIM = 256
BLK = MXU_DIM
VMEM_LIMIT_BYTES = 56 * 1024 * 1024


def _dot(a, b):
    return jnp.dot(a, b, preferred_element_type=F32)


def _dot_nt(a, b):
    return lax.dot_general(a, b, (((1,), (1,)), ((), ())), preferred_element_type=F32)


def _dot_tn(a, b):
    return lax.dot_general(a, b, (((0,), (0,)), ((), ())), preferred_element_type=F32)


def _softplus(x):
    return jnp.maximum(x, 0.0) + jnp.log1p(jnp.exp(-jnp.abs(x)))


def _softplus_fast(x):
    one = jnp.ones((), x.dtype)
    return jnp.maximum(x, 0 * one) + jnp.log(one + jnp.exp2(jnp.abs(x) * (-LOG2E * one)))


def _sigmoid(x):
    return 1.0 / (1.0 + jnp.exp(-x))


def _rms_rows(x, gain):
    ms = jnp.mean(x * x, axis=-1, keepdims=True)
    return x * lax.rsqrt(ms + RMS_EPS) * gain


def _split3(x):
    p0 = x.astype(BF16)
    r1 = x - p0.astype(F32)
    p1 = r1.astype(BF16)
    p2 = (r1 - p1.astype(F32)).astype(BF16)
    return p0, p1, p2


def _resident(shape):
    return pl.BlockSpec(shape, lambda *_: (0,) * len(shape), pipeline_mode=pl.Buffered(1))


def _resident_layer(stacked, layer):
    rest = stacked.shape[1:]
    return pl.BlockSpec((None,) + rest, lambda *_: (layer,) + (0,) * len(rest), pipeline_mode=pl.Buffered(1))


def _params(semantics):
    return pltpu.CompilerParams(dimension_semantics=semantics, vmem_limit_bytes=VMEM_LIMIT_BYTES)


def _ffn_body(x_ref, gain_ref, win_ref, wout_ref, o_ref, *, d_ff):
    x = x_ref[...]
    xn = _rms_rows(x, gain_ref[...]).astype(BF16)
    h = _dot(xn, win_ref[...])
    gate = h[:, :d_ff]
    up = h[:, d_ff:]
    act = (gate * _sigmoid(gate) * up).astype(BF16)
    o_ref[...] = x + 0.5 * _dot(act, wout_ref[...])


def _ffn(x, gain, w_in, w_out, layer, *, tm):
    m, d = x.shape
    d_ff = w_out.shape[1]
    return pl.pallas_call(
        functools.partial(_ffn_body, d_ff=d_ff),
        grid=(m // tm,),
        in_specs=[
            pl.BlockSpec((tm, d), lambda i: (i, 0)),
            _resident((1, d)),
            _resident_layer(w_in, layer),
            _resident_layer(w_out, layer),
        ],
        out_specs=pl.BlockSpec((tm, d), lambda i: (i, 0)),
        out_shape=jax.ShapeDtypeStruct((m, d), F32),
        compiler_params=_params(("parallel",)),
        name="ffn",
    )(x, gain, w_in, w_out)


def _proj_body(x_ref, gain_ref, wdn_ref, wsb_ref, wba_ref, qgain_ref, kgain_ref, avec_ref, dtvec_ref,
               dnqkv_ref, z_ref, qb_ref, kb_ref, vb_ref, ga_ref, gb_ref, ba_ref, *, d, heads):
    x = x_ref[...]
    xn = _rms_rows(x, gain_ref[...]).astype(BF16)
    dnqkv_ref[...] = _dot(xn, wdn_ref[:, 0:3 * d])
    z_ref[...] = _dot(xn, wdn_ref[:, 3 * d:4 * d])
    sq = _dot(xn, wsb_ref[:, 0:d])
    sk = _dot(xn, wsb_ref[:, d:2 * d])
    qgain = qgain_ref[...] * (HEAD_DIM ** -0.5)
    kgain = kgain_ref[...]
    for h in range(heads):
        sl = slice(h * HEAD_DIM, (h + 1) * HEAD_DIM)
        qb_ref[:, sl] = _rms_rows(sq[:, sl], qgain).astype(BF16)
        kb_ref[:, sl] = _rms_rows(sk[:, sl], kgain).astype(BF16)
    vb_ref[...] = _dot(xn, wsb_ref[:, 2 * d:3 * d]).astype(BF16)
    ga_ref[...] = _dot(xn, wsb_ref[:, 3 * d:4 * d])
    gb_ref[...] = _dot(xn, wsb_ref[:, 4 * d:5 * d])
    pba = _dot(xn, wba_ref[...])
    lane = lax.broadcasted_iota(jnp.int32, pba.shape, 1)
    beta = _sigmoid(pba)
    logdecay = avec_ref[...] * _softplus(pba + dtvec_ref[...])
    ba_ref[...] = jnp.where(lane < heads, beta, logdecay)


def _proj(x, gain, wdn, wsb, wba, layer, qgain, kgain, avec, dtvec, *, tm, heads):
    m, d = x.shape
    row = lambda i: (i, 0)
    f32_out = lambda n: jax.ShapeDtypeStruct((m, n), F32)
    bf_out = lambda n: jax.ShapeDtypeStruct((m, n), BF16)
    return pl.pallas_call(
        functools.partial(_proj_body, d=d, heads=heads),
        grid=(m // tm,),
        in_specs=[
            pl.BlockSpec((tm, d), row),
            _resident((1, d)),
            _resident_layer(wdn, layer),
            _resident_layer(wsb, layer),
            _resident_layer(wba, layer),
            _resident((1, HEAD_DIM)),
            _resident((1, HEAD_DIM)),
            _resident((1, HEAD_DIM)),
            _resident((1, HEAD_DIM)),
        ],
        out_specs=[
            pl.BlockSpec((tm, 3 * d), row),
            pl.BlockSpec((tm, d), row),
            pl.BlockSpec((tm, d), row),
            pl.BlockSpec((tm, d), row),
            pl.BlockSpec((tm, d), row),
            pl.BlockSpec((tm, d), row),
            pl.BlockSpec((tm, d), row),
            pl.BlockSpec((tm, HEAD_DIM), row),
        ],
        out_shape=[f32_out(3 * d), f32_out(d), bf_out(d), bf_out(d), bf_out(d),
                   f32_out(d), f32_out(d), f32_out(HEAD_DIM)],
        compiler_params=_params(("parallel",)),
        name="proj",
    )(x, gain, wdn, wsb, wba, qgain, kgain, avec, dtvec)


def _conv_silu(x_ref, w_ref, pre_ref, act_ref, idx, sl):
    rows = BLK // DN_CONV
    pre_ref[idx, 8:, :] = x_ref[:, sl]
    w = w_ref[:, sl]
    for r in range(DN_CONV):
        y = None
        for i in range(DN_CONV):
            window = pre_ref[idx, pl.ds(8 + r - i, rows, stride=DN_CONV), :]
            term = window * w[DN_CONV - 1 - i:DN_CONV - i, :]
            y = term if y is None else y + term
        act_ref[idx, pl.ds(r, rows, stride=DN_CONV), :] = y * _sigmoid(y)
    pre_ref[idx, 0:8, :] = x_ref[BLK - 8:, sl]


def _column(tile, lane_idx):
    lane = lax.broadcasted_iota(jnp.int32, tile.shape, 1)
    return jnp.sum(jnp.where(lane == lane_idx, tile, 0.0), axis=1, keepdims=True)


def _inverse_masks(row, col):
    shift = INV_BASE.bit_length() - 1
    blk = lambda x, s: lax.shift_right_logical(x, s)
    base = blk(row, shift) == blk(col, shift)
    joins = []
    while (1 << shift) < BLK:
        joins.append((blk(row, shift + 1) == blk(col, shift + 1)) & (blk(row, shift) != blk(col, shift)))
        shift += 1
    return base, joins


def _unit_lower_inverse(lmats, eye, masks):
    base, joins = masks
    blocks = [jnp.where(base, l, 0.0) for l in lmats]
    invs = [eye - b for b in blocks]
    powers = [b.astype(BF16) for b in blocks]
    span = 2
    while span < INV_BASE:
        powers = [_dot(p, p).astype(BF16) for p in powers]
        invs = [t + _dot(t.astype(BF16), p) for t, p in zip(invs, powers)]
        span *= 2
    for join in joins[:-1]:
        t16s = [t.astype(BF16) for t in invs]
        cts = [_dot(jnp.where(join, l, 0.0).astype(BF16), t16).astype(BF16) for l, t16 in zip(lmats, t16s)]
        invs = [t - _dot(t16, ct) for t, t16, ct in zip(invs, t16s, cts)]
    return invs


def _unit_lower_solve(lmats, rhss, eye, masks):
    half = BLK // 2
    invs = _unit_lower_inverse(lmats, eye, masks)
    t11s = [t[:half, :half].astype(BF16) for t in invs]
    t22s = [t[half:, half:].astype(BF16) for t in invs]
    tops = [_dot(t11, r[:half].astype(BF16)) for t11, r in zip(t11s, rhss)]
    lows = [r[half:] - _dot(l[half:, :half].astype(BF16), top.astype(BF16)) for l, r, top in zip(lmats, rhss, tops)]
    bottoms = [_dot(t22, low.astype(BF16)) for t22, low in zip(t22s, lows)]
    return [jnp.concatenate([top, bottom], axis=0) for top, bottom in zip(tops, bottoms)]


def _dn_body(xq_ref, xk_ref, xv_ref, wq_ref, wk_ref, wv_ref, ba_ref, z_ref, ogain_ref,
             o_ref, state_ref, pre_ref, act_ref, gct_ref, *, hg, heads):
    t = pl.program_id(2)
    group = pl.program_id(1)
    js = range(hg)
    lanes = [slice(j * HEAD_DIM, (j + 1) * HEAD_DIM) for j in js]

    @pl.when(t == 0)
    def _():
        state_ref[...] = jnp.zeros_like(state_ref)
        pre_ref[:, 0:8, :] = jnp.zeros((3 * hg, 8, HEAD_DIM), F32)

    for part, (x_ref, w_ref) in enumerate(((xq_ref, wq_ref), (xk_ref, wk_ref), (xv_ref, wv_ref))):
        for j in js:
            _conv_silu(x_ref, w_ref, pre_ref, act_ref, part * hg + j, lanes[j])

    row = lax.broadcasted_iota(jnp.int32, (BLK, BLK), 0)
    col = lax.broadcasted_iota(jnp.int32, (BLK, BLK), 1)
    lower_incl = row >= col
    strict = row > col
    eye = jnp.where(row == col, 1.0, 0.0)
    masks = _inverse_masks(row, col)
    cum_mat = jnp.where(lower_incl, 1.0, 0.0).astype(BF16)

    ba = ba_ref[...]
    g0, g1, g2 = _split3(ba)
    gc_all = _dot(cum_mat, g0) + _dot(cum_mat, g1) + _dot(cum_mat, g2)
    gl_all = gc_all[BLK - 1:BLK, :]
    gct_ref[...] = gc_all.T
    egc_all = jnp.exp(gc_all)
    ekd_all = jnp.exp(gl_all - gc_all)
    egl_all = jnp.broadcast_to(jnp.exp(gl_all), (HEAD_DIM, HEAD_DIM))

    ogain = ogain_ref[...]
    hs = [group * hg + j for j in js]
    l2 = lambda x: lax.rsqrt(jnp.sum(x * x, axis=-1, keepdims=True) + L2_EPS)
    qs = [act_ref[j] * (l2(act_ref[j]) * (HEAD_DIM ** -0.5)) for j in js]
    ks = [act_ref[hg + j] * l2(act_ref[hg + j]) for j in js]
    vs = [act_ref[2 * hg + j] for j in js]
    betas = [_column(ba, h) for h in hs]
    egcs = [_column(egc_all, heads + h) for h in hs]
    ekds = [_column(ekd_all, heads + h) for h in hs]
    egls = [_column(egl_all, heads + h) for h in hs]
    decays = [jnp.exp(jnp.where(lower_incl,
                                _column(gc_all, heads + h) - gct_ref[pl.ds(heads + h, 1), :], NEG_BIG))
              for h in hs]
    kbs = [k * b for k, b in zip(ks, betas)]
    k16s = [k.astype(BF16) for k in ks]
    lmats = [jnp.where(strict, _dot_nt(kb.astype(BF16), k16) * dec, 0.0)
             for kb, k16, dec in zip(kbs, k16s, decays)]
    rhss = [jnp.concatenate([v * b, kb * egc], axis=1)
            for v, b, kb, egc in zip(vs, betas, kbs, egcs)]
    sols = _unit_lower_solve(lmats, rhss, eye, masks)
    attn16s = [(_dot_nt(q.astype(BF16), k16) * dec).astype(BF16) for q, k16, dec in zip(qs, k16s, decays)]
    qd16s = [(q * egc).astype(BF16) for q, egc in zip(qs, egcs)]
    kd16s = [(k * ekd).astype(BF16) for k, ekd in zip(ks, ekds)]
    states = [state_ref[j] for j in js]
    stacked = [jnp.concatenate([sol[:, HEAD_DIM:].astype(BF16), qd16], axis=0)
               for sol, qd16 in zip(sols, qd16s)]
    s16s = [s.astype(BF16) for s in states]
    if hg % 2 == 0:
        zero = jnp.zeros((HEAD_DIM, HEAD_DIM), BF16)
        ws_qss = []
        for a in range(0, hg, 2):
            both = _dot(jnp.concatenate([stacked[a], stacked[a + 1]], axis=1),
                        jnp.concatenate([jnp.concatenate([s16s[a], zero], axis=1),
                                         jnp.concatenate([zero, s16s[a + 1]], axis=1)], axis=0))
            ws_qss += [both[:, :HEAD_DIM], both[:, HEAD_DIM:]]
    else:
        ws_qss = [_dot(lhs, s16) for lhs, s16 in zip(stacked, s16s)]
    vnew16s = [(sol[:, :HEAD_DIM] - ws_qs[:BLK]).astype(BF16) for sol, ws_qs in zip(sols, ws_qss)]
    outs = [ws_qs[BLK:] + _dot(attn16, vnew16) for ws_qs, attn16, vnew16 in zip(ws_qss, attn16s, vnew16s)]
    for j, s, egl, kd16, vnew16 in zip(js, states, egls, kd16s, vnew16s):
        state_ref[j] = s * egl + _dot_tn(kd16, vnew16)
    for sl, o in zip(lanes, outs):
        zt = z_ref[:, sl]
        o_ref[:, sl] = (_rms_rows(o, ogain) * (zt * _sigmoid(zt))).astype(BF16)


def _deltanet(dnqkv, conv_w, layer, ba, z, ogain, *, batch, seq, heads, hg):
    m = dnqkv.shape[0]
    d = heads * HEAD_DIM
    nt = seq // BLK
    ng = heads // hg
    wide = hg * HEAD_DIM
    rows = lambda b, g, t: b * nt + t
    x_part = lambda part: pl.BlockSpec((BLK, wide), lambda b, g, t: (rows(b, g, t), part * ng + g))
    w_part = lambda part: pl.BlockSpec((None, DN_CONV, wide), lambda b, g, t: (layer, 0, part * ng + g))
    return pl.pallas_call(
        functools.partial(_dn_body, hg=hg, heads=heads),
        grid=(batch, ng, nt),
        in_specs=[
            x_part(0), x_part(1), x_part(2),
            w_part(0), w_part(1), w_part(2),
            pl.BlockSpec((BLK, HEAD_DIM), lambda b, g, t: (rows(b, g, t), 0)),
            pl.BlockSpec((BLK, wide), lambda b, g, t: (rows(b, g, t), g)),
            pl.BlockSpec((1, HEAD_DIM), lambda b, g, t: (0, 0)),
        ],
        out_specs=pl.BlockSpec((BLK, wide), lambda b, g, t: (rows(b, g, t), g)),
        out_shape=jax.ShapeDtypeStruct((m, d), BF16),
        scratch_shapes=[
            pltpu.VMEM((hg, HEAD_DIM, HEAD_DIM), F32),
            pltpu.VMEM((3 * hg, 8 + BLK, HEAD_DIM), F32),
            pltpu.VMEM((3 * hg, BLK, HEAD_DIM), F32),
            pltpu.VMEM((HEAD_DIM, BLK), F32),
        ],
        compiler_params=_params(("parallel", "parallel", "arbitrary")),
        name="deltanet",
    )(dnqkv, dnqkv, dnqkv, conv_w, conv_w, conv_w, ba, z, ogain)


def _sb_body(qa_ref, qb_ref, k_ref, v_ref, o_ref, qs_ref, acc_ref, right_ref, *, hs, nq):
    p = pl.program_id(2)
    long_blk = nq - 1 - p
    row = lax.broadcasted_iota(jnp.int32, (BLK, BLK), 0)
    col = lax.broadcasted_iota(jnp.int32, (BLK, BLK), 1)
    suffix = jnp.where(row >= col, 1.0, 0.0).astype(BF16)
    causal = col < row
    lanes = [slice(h * HEAD_DIM, (h + 1) * HEAD_DIM) for h in range(hs)]
    qs_ref[0] = qa_ref[...]
    qs_ref[1] = qb_ref[...]

    tiles = [(0, long_blk, True), (1, p, True)]
    for m in range(nq - 1):
        in_short = m >= long_blk
        tiles.append((jnp.where(in_short, 1, 0), jnp.where(in_short, p - 1 - (m - long_blk), long_blk - 1 - m), False))

    def scores(slot, kblk):
        start = pl.multiple_of(kblk * BLK, BLK)
        return [_dot_nt(qs_ref[slot, :, sl], k_ref[pl.ds(start, BLK), sl]) for sl in lanes]

    def cumsums(zs, diagonal):
        sps = [_softplus_fast(z.astype(BF16)) for z in zs]
        if diagonal:
            sps = [jnp.where(causal, sp, jnp.zeros_like(sp)) for sp in sps]
        return [_dot(sp, suffix) for sp in sps]

    def accumulate(slot, kblk, diagonal, zs, cums):
        start = pl.multiple_of(kblk * BLK, BLK)
        if diagonal:
            ws = [jnp.where(causal, jnp.exp(z - cum), 0.0) for z, cum in zip(zs, cums)]
        else:
            rights = [right_ref[slot, h] for h in range(hs)]
            ws = [jnp.exp(z - cum - jnp.concatenate([r, r], axis=1)) for z, cum, r in zip(zs, cums, rights)]
        pvs = [_dot(w.astype(BF16), v_ref[pl.ds(start, BLK), sl]) for w, sl in zip(ws, lanes)]
        totals = [jnp.broadcast_to(cum[:, 0:1], (BLK, HEAD_DIM)) for cum in cums]
        for h in range(hs):
            if diagonal:
                acc_ref[slot, h] = pvs[h]
                right_ref[slot, h] = totals[h]
            else:
                acc_ref[slot, h] = acc_ref[slot, h] + pvs[h]
                right_ref[slot, h] = rights[h] + totals[h]

    n_tiles = len(tiles)
    zs = {0: scores(*tiles[0][:2]), 1: scores(*tiles[1][:2])}
    cums = {0: cumsums(zs[0], tiles[0][2])}
    for n in range(n_tiles):
        if n + 2 < n_tiles:
            zs[n + 2] = scores(*tiles[n + 2][:2])
        if n + 1 < n_tiles:
            cums[n + 1] = cumsums(zs[n + 1], tiles[n + 1][2])
        accumulate(*tiles[n], zs.pop(n), cums.pop(n))

    for slot in range(2):
        for h, sl in enumerate(lanes):
            o_ref[slot, :, sl] = acc_ref[slot, h].astype(BF16)


def _stick_breaking(qb, kb, vb, *, batch, seq, heads, hs):
    m, d = qb.shape
    nq = seq // BLK
    assert nq % 2 == 0
    wide = hs * HEAD_DIM
    return pl.pallas_call(
        functools.partial(_sb_body, hs=hs, nq=nq),
        grid=(batch, heads // hs, nq // 2),
        in_specs=[
            pl.BlockSpec((BLK, wide), lambda b, h, p: (b * nq + (nq - 1 - p), h)),
            pl.BlockSpec((BLK, wide), lambda b, h, p: (b * nq + p, h)),
            pl.BlockSpec((seq, wide), lambda b, h, p: (b, h)),
            pl.BlockSpec((seq, wide), lambda b, h, p: (b, h)),
        ],
        out_specs=pl.BlockSpec((None, None, 2, BLK, wide), lambda b, h, p: (b, p, 0, 0, h)),
        out_shape=jax.ShapeDtypeStruct((batch, nq // 2, 2, BLK, d), BF16),
        scratch_shapes=[
            pltpu.VMEM((2, BLK, wide), BF16),
            pltpu.VMEM((2, hs, BLK, HEAD_DIM), F32),
            pltpu.VMEM((2, hs, BLK, HEAD_DIM), F32),
        ],
        compiler_params=_params(("parallel", "parallel", "arbitrary")),
        name="stickbrk",
    )(qb, qb, kb, vb)


def _merge_body(x_ref, oa_ref, ob0_ref, ob1_ref, ga_ref, gb_ref, wa_ref, wb_ref, wo_ref, o_ref):
    ob = jnp.concatenate([ob0_ref[...], ob1_ref[...]], axis=0)
    ya = _dot(oa_ref[...], wa_ref[...])
    yb = _dot(ob, wb_ref[...])
    merged = _sigmoid(ga_ref[...]) * ya + _sigmoid(gb_ref[...]) * yb
    o_ref[...] = x_ref[...] + _dot(merged.astype(BF16), wo_ref[...])


def _merge(x, oa, ob, ga, gb, wa, wb, wo, layer, *, nq):
    m, d = x.shape
    tile = pl.BlockSpec((2 * BLK, d), lambda i: (i, 0))

    def ob_block(which):
        def index(i):
            blk = lax.rem(2 * i + which, nq)
            return (lax.div(2 * i + which, nq), jnp.minimum(blk, nq - 1 - blk), jnp.where(blk < nq // 2, 1, 0), 0, 0)
        return pl.BlockSpec((None, None, None, BLK, d), index)

    return pl.pallas_call(
        _merge_body,
        grid=(m // (2 * BLK),),
        in_specs=[tile, tile, ob_block(0), ob_block(1), tile, tile,
                  _resident_layer(wa, layer), _resident_layer(wb, layer), _resident_layer(wo, layer)],
        out_specs=tile,
        out_shape=jax.ShapeDtypeStruct((m, d), F32),
        compiler_params=_params(("parallel",)),
        name="merge",
    )(x, oa, ob, ob, ga, gb, wa, wb, wo)


def _row_tile(m, want):
    tm = min(want, m)
    assert m % tm == 0, (m, tm)
    return tm


def kernel(x, ffn1_norm, ffn1_w_in, ffn1_w_out, mix_norm, w_in, dn_conv_w, dn_a_log, dn_dt_bias,
           dn_out_norm, sb_q_norm, sb_k_norm, w_branch_a, w_branch_b, w_out,
           ffn2_norm, ffn2_w_in, ffn2_w_out):
    batch, seq, d = x.shape
    depth = ffn1_norm.shape[0]
    heads = d // HEAD_DIM
    assert seq % (2 * BLK) == 0 and d % HEAD_DIM == 0 and 2 * heads <= HEAD_DIM
    m = batch * seq
    tm_ffn = _row_tile(m, 512)
    tm_proj = _row_tile(m, 256)
    hg = 8 if heads % 8 == 0 else 1
    hs = 4 if heads % 4 == 0 else 1

    n_ba = 2 * heads
    w_dn = w_in[:, :, :4 * d].astype(BF16)
    w_sb = w_in[:, :, 4 * d + n_ba:].astype(BF16)
    w_ba = jnp.pad(w_in[:, :, 4 * d:4 * d + n_ba], ((0, 0), (0, 0), (0, HEAD_DIM - n_ba))).astype(BF16)
    lane_pad = lambda v: jnp.pad(v, ((0, 0), (heads, HEAD_DIM - n_ba)))[:, None, :]
    avec = lane_pad(-jnp.exp(dn_a_log.astype(F32)))
    dtvec = lane_pad(dn_dt_bias.astype(F32))
    f1_in, f1_out = ffn1_w_in.astype(BF16), ffn1_w_out.astype(BF16)
    f2_in, f2_out = ffn2_w_in.astype(BF16), ffn2_w_out.astype(BF16)
    wa, wb, wo = w_branch_a.astype(BF16), w_branch_b.astype(BF16), w_out.astype(BF16)

    xf = x.reshape(m, d)
    for l in range(depth):
        xf = _ffn(xf, ffn1_norm[l][None], f1_in, f1_out, l, tm=tm_ffn)
        dnqkv, z, qb, kb, vb, ga, gb, ba = _proj(
            xf, mix_norm[l][None], w_dn, w_sb, w_ba, l, sb_q_norm[l][None], sb_k_norm[l][None],
            avec[l], dtvec[l], tm=tm_proj, heads=heads)
        oa = _deltanet(dnqkv, dn_conv_w, l, ba, z, dn_out_norm[l][None],
                       batch=batch, seq=seq, heads=heads, hg=hg)
        ob = _stick_breaking(qb, kb, vb, batch=batch, seq=seq, heads=heads, hs=hs)
        xf = _merge(xf, oa, ob, ga, gb, wa, wb, wo, l, nq=seq // BLK)
        xf = _ffn(xf, ffn2_norm[l][None], f2_in, f2_out, l, tm=tm_ffn)
    return xf.reshape(batch, seq, d)
```

```python
import functools

import jax
import jax.numpy as jnp
from jax import lax
from jax.experimental import pallas as pl
from jax.experimental.pallas import tpu as pltpu

F32 = jnp.float32
BF16 = jnp.bfloat16

HEAD_DIM = 128
DN_CONV = 4
INV_BASE = 8
RMS_EPS = 1e-6
L2_EPS = 1e-6
NEG_BIG = -1e30
LOG2E = 1.4426950408889634

MXU_DIM = 256
BLK = MXU_DIM
VMEM_LIMIT_BYTES = 56 * 1024 * 1024


def _dot(a, b):
    return jnp.dot(a, b, preferred_element_type=F32)


def _dot_nt(a, b):
    return lax.dot_general(a, b, (((1,), (1,)), ((), ())), preferred_element_type=F32)


def _dot_tn(a, b):
    return lax.dot_general(a, b, (((0,), (0,)), ((), ())), preferred_element_type=F32)


def _softplus(x):
    return jnp.maximum(x, 0.0) + jnp.log1p(jnp.exp(-jnp.abs(x)))


def _softplus_fast(x):
    one = jnp.ones((), x.dtype)
    return jnp.maximum(x, 0 * one) + jnp.log(one + jnp.exp2(jnp.abs(x) * (-LOG2E * one)))


def _sigmoid(x):
    return 1.0 / (1.0 + jnp.exp(-x))


def _rms_rows(x, gain):
    ms = jnp.mean(x * x, axis=-1, keepdims=True)
    return x * lax.rsqrt(ms + RMS_EPS) * gain


def _split3(x):
    p0 = x.astype(BF16)
    r1 = x - p0.astype(F32)
    p1 = r1.astype(BF16)
    p2 = (r1 - p1.astype(F32)).astype(BF16)
    return p0, p1, p2


def _resident(shape):
    return pl.BlockSpec(shape, lambda *_: (0,) * len(shape), pipeline_mode=pl.Buffered(1))


def _resident_layer(stacked, layer):
    rest = stacked.shape[1:]
    return pl.BlockSpec((None,) + rest, lambda *_: (layer,) + (0,) * len(rest), pipeline_mode=pl.Buffered(1))


def _params(semantics):
    return pltpu.CompilerParams(dimension_semantics=semantics, vmem_limit_bytes=VMEM_LIMIT_BYTES)


def _ffn_body(x_ref, gain_ref, win_ref, wout_ref, o_ref, *, d_ff):
    x = x_ref[...]
    xn = _rms_rows(x, gain_ref[...]).astype(BF16)
    h = _dot(xn, win_ref[...])
    gate = h[:, :d_ff]
    up = h[:, d_ff:]
    act = (gate * _sigmoid(gate) * up).astype(BF16)
    o_ref[...] = x + 0.5 * _dot(act, wout_ref[...])


def _ffn(x, gain, w_in, w_out, layer, *, tm):
    m, d = x.shape
    d_ff = w_out.shape[1]
    return pl.pallas_call(
        functools.partial(_ffn_body, d_ff=d_ff),
        grid=(m // tm,),
        in_specs=[
            pl.BlockSpec((tm, d), lambda i: (i, 0)),
            _resident((1, d)),
            _resident_layer(w_in, layer),
            _resident_layer(w_out, layer),
        ],
        out_specs=pl.BlockSpec((tm, d), lambda i: (i, 0)),
        out_shape=jax.ShapeDtypeStruct((m, d), F32),
        compiler_params=_params(("parallel",)),
        name="ffn",
    )(x, gain, w_in, w_out)


def _proj_body(x_ref, gain_ref, wdn_ref, wsb_ref, wba_ref, qgain_ref, kgain_ref, avec_ref, dtvec_ref,
               dnqkv_ref, z_ref, qb_ref, kb_ref, vb_ref, ga_ref, gb_ref, ba_ref, *, d, heads):
    x = x_ref[...]
    xn = _rms_rows(x, gain_ref[...]).astype(BF16)
    dnqkv_ref[...] = _dot(xn, wdn_ref[:, 0:3 * d])
    z_ref[...] = _dot(xn, wdn_ref[:, 3 * d:4 * d])
    sq = _dot(xn, wsb_ref[:, 0:d])
    sk = _dot(xn, wsb_ref[:, d:2 * d])
    qgain = qgain_ref[...] * (HEAD_DIM ** -0.5)
    kgain = kgain_ref[...]
    for h in range(heads):
        sl = slice(h * HEAD_DIM, (h + 1) * HEAD_DIM)
        qb_ref[:, sl] = _rms_rows(sq[:, sl], qgain).astype(BF16)
        kb_ref[:, sl] = _rms_rows(sk[:, sl], kgain).astype(BF16)
    vb_ref[...] = _dot(xn, wsb_ref[:, 2 * d:3 * d]).astype(BF16)
    ga_ref[...] = _dot(xn, wsb_ref[:, 3 * d:4 * d])
    gb_ref[...] = _dot(xn, wsb_ref[:, 4 * d:5 * d])
    pba = _dot(xn, wba_ref[...])
    lane = lax.broadcasted_iota(jnp.int32, pba.shape, 1)
    beta = _sigmoid(pba)
    logdecay = avec_ref[...] * _softplus(pba + dtvec_ref[...])
    ba_ref[...] = jnp.where(lane < heads, beta, logdecay)


def _proj(x, gain, wdn, wsb, wba, layer, qgain, kgain, avec, dtvec, *, tm, heads):
    m, d = x.shape
    row = lambda i: (i, 0)
    f32_out = lambda n: jax.ShapeDtypeStruct((m, n), F32)
    bf_out = lambda n: jax.ShapeDtypeStruct((m, n), BF16)
    return pl.pallas_call(
        functools.partial(_proj_body, d=d, heads=heads),
        grid=(m // tm,),
        in_specs=[
            pl.BlockSpec((tm, d), row),
            _resident((1, d)),
            _resident_layer(wdn, layer),
            _resident_layer(wsb, layer),
            _resident_layer(wba, layer),
            _resident((1, HEAD_DIM)),
            _resident((1, HEAD_DIM)),
            _resident((1, HEAD_DIM)),
            _resident((1, HEAD_DIM)),
        ],
        out_specs=[
            pl.BlockSpec((tm, 3 * d), row),
            pl.BlockSpec((tm, d), row),
            pl.BlockSpec((tm, d), row),
            pl.BlockSpec((tm, d), row),
            pl.BlockSpec((tm, d), row),
            pl.BlockSpec((tm, d), row),
            pl.BlockSpec((tm, d), row),
            pl.BlockSpec((tm, HEAD_DIM), row),
        ],
        out_shape=[f32_out(3 * d), f32_out(d), bf_out(d), bf_out(d), bf_out(d),
                   f32_out(d), f32_out(d), f32_out(HEAD_DIM)],
        compiler_params=_params(("parallel",)),
        name="proj",
    )(x, gain, wdn, wsb, wba, qgain, kgain, avec, dtvec)


def _conv_silu(x_ref, w_ref, pre_ref, act_ref, idx, sl):
    rows = BLK // DN_CONV
    pre_ref[idx, 8:, :] = x_ref[:, sl]
    w = w_ref[:, sl]
    for r in range(DN_CONV):
        y = None
        for i in range(DN_CONV):
            window = pre_ref[idx, pl.ds(8 + r - i, rows, stride=DN_CONV), :]
            term = window * w[DN_CONV - 1 - i:DN_CONV - i, :]
            y = term if y is None else y + term
        act_ref[idx, pl.ds(r, rows, stride=DN_CONV), :] = y * _sigmoid(y)
    pre_ref[idx, 0:8, :] = x_ref[BLK - 8:, sl]


def _column(tile, lane_idx):
    lane = lax.broadcasted_iota(jnp.int32, tile.shape, 1)
    return jnp.sum(jnp.where(lane == lane_idx, tile, 0.0), axis=1, keepdims=True)


def _inverse_masks(row, col):
    shift = INV_BASE.bit_length() - 1
    blk = lambda x, s: lax.shift_right_logical(x, s)
    base = blk(row, shift) == blk(col, shift)
    joins = []
    while (1 << shift) < BLK:
        joins.append((blk(row, shift + 1) == blk(col, shift + 1)) & (blk(row, shift) != blk(col, shift)))
        shift += 1
    return base, joins


def _unit_lower_inverse(lmats, eye, masks):
    base, joins = masks
    blocks = [jnp.where(base, l, 0.0) for l in lmats]
    invs = [eye - b for b in blocks]
    powers = [b.astype(BF16) for b in blocks]
    span = 2
    while span < INV_BASE:
        powers = [_dot(p, p).astype(BF16) for p in powers]
        invs = [t + _dot(t.astype(BF16), p) for t, p in zip(invs, powers)]
        span *= 2
    l16s = [l.astype(BF16) for l in lmats]
    t16s = [t.astype(BF16) for t in invs]
    for join in joins[:-1]:
        pick = jnp.where(join, 1.0, 0.0).astype(BF16)
        cts = [_dot(l16 * pick, t16).astype(BF16) for l16, t16 in zip(l16s, t16s)]
        t16s = [t16 - _dot(t16, ct).astype(BF16) for t16, ct in zip(t16s, cts)]
    return t16s


def _unit_lower_solve(lmats, rhss, eye, masks):
    half = BLK // 2
    t16s = _unit_lower_inverse(lmats, eye, masks)
    tops = [_dot(t[:half, :half], r[:half].astype(BF16)) for t, r in zip(t16s, rhss)]
    lows = [r[half:] - _dot(l[half:, :half].astype(BF16), top.astype(BF16)) for l, r, top in zip(lmats, rhss, tops)]
    bottoms = [_dot(t[half:, half:], low.astype(BF16)) for t, low in zip(t16s, lows)]
    return [jnp.concatenate([top, bottom], axis=0) for top, bottom in zip(tops, bottoms)]


def _dn_body(xq_ref, xk_ref, xv_ref, wq_ref, wk_ref, wv_ref, ba_ref, z_ref, ogain_ref,
             o_ref, state_ref, pre_ref, act_ref, gct_ref, *, hg, heads):
    t = pl.program_id(2)
    group = pl.program_id(1)
    js = range(hg)
    lanes = [slice(j * HEAD_DIM, (j + 1) * HEAD_DIM) for j in js]

    @pl.when(t == 0)
    def _():
        state_ref[...] = jnp.zeros_like(state_ref)
        pre_ref[:, 0:8, :] = jnp.zeros((3 * hg, 8, HEAD_DIM), F32)

    for part, (x_ref, w_ref) in enumerate(((xq_ref, wq_ref), (xk_ref, wk_ref), (xv_ref, wv_ref))):
        for j in js:
            _conv_silu(x_ref, w_ref, pre_ref, act_ref, part * hg + j, lanes[j])

    row = lax.broadcasted_iota(jnp.int32, (BLK, BLK), 0)
    col = lax.broadcasted_iota(jnp.int32, (BLK, BLK), 1)
    lower_incl = row >= col
    strict = row > col
    eye = jnp.where(row == col, 1.0, 0.0)
    masks = _inverse_masks(row, col)
    cum_mat = jnp.where(lower_incl, 1.0, 0.0).astype(BF16)

    ba = ba_ref[...]
    g0, g1, g2 = _split3(ba)
    gc_all = _dot(cum_mat, g0) + _dot(cum_mat, g1) + _dot(cum_mat, g2)
    gl_all = gc_all[BLK - 1:BLK, :]
    gct_ref[...] = gc_all.T
    egc_all = jnp.exp(gc_all)
    ekd_all = jnp.exp(gl_all - gc_all)
    egl_all = jnp.broadcast_to(jnp.exp(gl_all), (HEAD_DIM, HEAD_DIM))

    ogain = ogain_ref[...]
    hs = [group * hg + j for j in js]
    l2 = lambda x: lax.rsqrt(jnp.sum(x * x, axis=-1, keepdims=True) + L2_EPS)
    qs = [act_ref[j] * (l2(act_ref[j]) * (HEAD_DIM ** -0.5)) for j in js]
    ks = [act_ref[hg + j] * l2(act_ref[hg + j]) for j in js]
    vs = [act_ref[2 * hg + j] for j in js]
    betas = [_column(ba, h) for h in hs]
    egcs = [_column(egc_all, heads + h) for h in hs]
    ekds = [_column(ekd_all, heads + h) for h in hs]
    egls = [_column(egl_all, heads + h) for h in hs]
    decays = [jnp.exp(jnp.where(lower_incl,
                                _column(gc_all, heads + h) - gct_ref[pl.ds(heads + h, 1), :], NEG_BIG))
              for h in hs]
    kbs = [k * b for k, b in zip(ks, betas)]
    k16s = [k.astype(BF16) for k in ks]
    lmats = [jnp.where(strict, _dot_nt(kb.astype(BF16), k16) * dec, 0.0)
             for kb, k16, dec in zip(kbs, k16s, decays)]
    attn16s = [(_dot_nt(q.astype(BF16), k16) * dec).astype(BF16) for q, k16, dec in zip(qs, k16s, decays)]
    rhss = [jnp.concatenate([v * b, kb * egc], axis=1)
            for v, b, kb, egc in zip(vs, betas, kbs, egcs)]
    sols = _unit_lower_solve(lmats, rhss, eye, masks)
    qd16s = [(q * egc).astype(BF16) for q, egc in zip(qs, egcs)]
    kd16s = [(k * ekd).astype(BF16) for k, ekd in zip(ks, ekds)]
    states = [state_ref[j] for j in js]
    stacked = [jnp.concatenate([sol[:, HEAD_DIM:].astype(BF16), qd16], axis=0)
               for sol, qd16 in zip(sols, qd16s)]
    s16s = [s.astype(BF16) for s in states]
    if hg % 2 == 0:
        zero = jnp.zeros((HEAD_DIM, HEAD_DIM), BF16)
        ws_qss = []
        for a in range(0, hg, 2):
            both = _dot(jnp.concatenate([stacked[a], stacked[a + 1]], axis=1),
                        jnp.concatenate([jnp.concatenate([s16s[a], zero], axis=1),
                                         jnp.concatenate([zero, s16s[a + 1]], axis=1)], axis=0))
            ws_qss += [both[:, :HEAD_DIM], both[:, HEAD_DIM:]]
    else:
        ws_qss = [_dot(lhs, s16) for lhs, s16 in zip(stacked, s16s)]
    vnew16s = [(sol[:, :HEAD_DIM] - ws_qs[:BLK]).astype(BF16) for sol, ws_qs in zip(sols, ws_qss)]
    outs = [ws_qs[BLK:] + _dot(attn16, vnew16) for ws_qs, attn16, vnew16 in zip(ws_qss, attn16s, vnew16s)]
    for j, s, egl, kd16, vnew16 in zip(js, states, egls, kd16s, vnew16s):
        state_ref[j] = s * egl + _dot_tn(kd16, vnew16)
    for sl, o in zip(lanes, outs):
        zt = z_ref[:, sl]
        o_ref[:, sl] = (_rms_rows(o, ogain) * (zt * _sigmoid(zt))).astype(BF16)


def _deltanet(dnqkv, conv_w, layer, ba, z, ogain, *, batch, seq, heads, hg):
    m = dnqkv.shape[0]
    d = heads * HEAD_DIM
    nt = seq // BLK
    ng = heads // hg
    wide = hg * HEAD_DIM
    rows = lambda b, g, t: b * nt + t
    x_part = lambda part: pl.BlockSpec((BLK, wide), lambda b, g, t: (rows(b, g, t), part * ng + g))
    w_part = lambda part: pl.BlockSpec((None, DN_CONV, wide), lambda b, g, t: (layer, 0, part * ng + g))
    return pl.pallas_call(
        functools.partial(_dn_body, hg=hg, heads=heads),
        grid=(batch, ng, nt),
        in_specs=[
            x_part(0), x_part(1), x_part(2),
            w_part(0), w_part(1), w_part(2),
            pl.BlockSpec((BLK, HEAD_DIM), lambda b, g, t: (rows(b, g, t), 0)),
            pl.BlockSpec((BLK, wide), lambda b, g, t: (rows(b, g, t), g)),
            pl.BlockSpec((1, HEAD_DIM), lambda b, g, t: (0, 0)),
        ],
        out_specs=pl.BlockSpec((BLK, wide), lambda b, g, t: (rows(b, g, t), g)),
        out_shape=jax.ShapeDtypeStruct((m, d), BF16),
        scratch_shapes=[
            pltpu.VMEM((hg, HEAD_DIM, HEAD_DIM), F32),
            pltpu.VMEM((3 * hg, 8 + BLK, HEAD_DIM), F32),
            pltpu.VMEM((3 * hg, BLK, HEAD_DIM), F32),
            pltpu.VMEM((HEAD_DIM, BLK), F32),
        ],
        compiler_params=_params(("parallel", "parallel", "arbitrary")),
        name="deltanet",
    )(dnqkv, dnqkv, dnqkv, conv_w, conv_w, conv_w, ba, z, ogain)


def _sb_body(qa_ref, qb_ref, k_ref, v_ref, o_ref, qs_ref, acc_ref, right_ref, *, hs, nq):
    p = pl.program_id(2)
    long_blk = nq - 1 - p
    row = lax.broadcasted_iota(jnp.int32, (BLK, BLK), 0)
    col = lax.broadcasted_iota(jnp.int32, (BLK, BLK), 1)
    suffix = jnp.where(row >= col, 1.0, 0.0).astype(BF16)
    causal = col < row
    lanes = [slice(h * HEAD_DIM, (h + 1) * HEAD_DIM) for h in range(hs)]
    qs_ref[0] = qa_ref[...]
    qs_ref[1] = qb_ref[...]

    tiles = [(0, long_blk, True), (1, p, True)]
    for m in range(nq - 1):
        in_short = m >= long_blk
        tiles.append((jnp.where(in_short, 1, 0), jnp.where(in_short, p - 1 - (m - long_blk), long_blk - 1 - m), False))

    def scores(slot, kblk):
        start = pl.multiple_of(kblk * BLK, BLK)
        return [_dot_nt(qs_ref[slot, :, sl], k_ref[pl.ds(start, BLK), sl]) for sl in lanes]

    def cumsums(zs, diagonal):
        sps = [_softplus_fast(z.astype(BF16)) for z in zs]
        if diagonal:
            sps = [jnp.where(causal, sp, jnp.zeros_like(sp)) for sp in sps]
        return [_dot(sp, suffix) for sp in sps]

    def accumulate(slot, kblk, diagonal, zs, cums):
        start = pl.multiple_of(kblk * BLK, BLK)
        if diagonal:
            ws = [jnp.where(causal, jnp.exp(z - cum), 0.0) for z, cum in zip(zs, cums)]
        else:
            rights = [right_ref[slot, h] for h in range(hs)]
            ws = [jnp.exp(z - cum - jnp.concatenate([r, r], axis=1)) for z, cum, r in zip(zs, cums, rights)]
        pvs = [_dot(w.astype(BF16), v_ref[pl.ds(start, BLK), sl]) for w, sl in zip(ws, lanes)]
        totals = [jnp.broadcast_to(cum[:, 0:1], (BLK, HEAD_DIM)) for cum in cums]
        for h in range(hs):
            if diagonal:
                acc_ref[slot, h] = pvs[h]
                right_ref[slot, h] = totals[h]
            else:
                acc_ref[slot, h] = acc_ref[slot, h] + pvs[h]
                right_ref[slot, h] = rights[h] + totals[h]

    n_tiles = len(tiles)
    zs = {0: scores(*tiles[0][:2])}
    for n in range(n_tiles):
        if n + 1 < n_tiles:
            zs[n + 1] = scores(*tiles[n + 1][:2])
        z = zs.pop(n)
        accumulate(*tiles[n], z, cumsums(z, tiles[n][2]))

    for slot in range(2):
        for h, sl in enumerate(lanes):
            o_ref[slot, :, sl] = acc_ref[slot, h].astype(BF16)


def _stick_breaking(qb, kb, vb, *, batch, seq, heads, hs):
    m, d = qb.shape
    nq = seq // BLK
    assert nq % 2 == 0
    wide = hs * HEAD_DIM
    return pl.pallas_call(
        functools.partial(_sb_body, hs=hs, nq=nq),
        grid=(batch, heads // hs, nq // 2),
        in_specs=[
            pl.BlockSpec((BLK, wide), lambda b, h, p: (b * nq + (nq - 1 - p), h)),
            pl.BlockSpec((BLK, wide), lambda b, h, p: (b * nq + p, h)),
            pl.BlockSpec((seq, wide), lambda b, h, p: (b, h)),
            pl.BlockSpec((seq, wide), lambda b, h, p: (b, h)),
        ],
        out_specs=pl.BlockSpec((None, None, 2, BLK, wide), lambda b, h, p: (b, p, 0, 0, h)),
        out_shape=jax.ShapeDtypeStruct((batch, nq // 2, 2, BLK, d), BF16),
        scratch_shapes=[
            pltpu.VMEM((2, BLK, wide), BF16),
            pltpu.VMEM((2, hs, BLK, HEAD_DIM), F32),
            pltpu.VMEM((2, hs, BLK, HEAD_DIM), F32),
        ],
        compiler_params=_params(("parallel", "parallel", "arbitrary")),
        name="stickbrk",
    )(qb, qb, kb, vb)


def _merge_body(x_ref, oa_ref, ob0_ref, ob1_ref, ga_ref, gb_ref, wa_ref, wb_ref, wo_ref, o_ref):
    ob = jnp.concatenate([ob0_ref[...], ob1_ref[...]], axis=0)
    ya = _dot(oa_ref[...], wa_ref[...])
    yb = _dot(ob, wb_ref[...])
    merged = _sigmoid(ga_ref[...]) * ya + _sigmoid(gb_ref[...]) * yb
    o_ref[...] = x_ref[...] + _dot(merged.astype(BF16), wo_ref[...])


def _merge(x, oa, ob, ga, gb, wa, wb, wo, layer, *, nq):
    m, d = x.shape
    tile = pl.BlockSpec((2 * BLK, d), lambda i: (i, 0))

    def ob_block(which):
        def index(i):
            blk = lax.rem(2 * i + which, nq)
            return (lax.div(2 * i + which, nq), jnp.minimum(blk, nq - 1 - blk), jnp.where(blk < nq // 2, 1, 0), 0, 0)
        return pl.BlockSpec((None, None, None, BLK, d), index)

    return pl.pallas_call(
        _merge_body,
        grid=(m // (2 * BLK),),
        in_specs=[tile, tile, ob_block(0), ob_block(1), tile, tile,
                  _resident_layer(wa, layer), _resident_layer(wb, layer), _resident_layer(wo, layer)],
        out_specs=tile,
        out_shape=jax.ShapeDtypeStruct((m, d), F32),
        compiler_params=_params(("parallel",)),
        name="merge",
    )(x, oa, ob, ob, ga, gb, wa, wb, wo)


def _regroup_body(w_ref, dn_ref, sb_ref, ba_ref, *, d, n_ba):
    rows = w_ref.shape[0]
    dn_ref[...] = w_ref[:, 0:4 * d].astype(BF16)
    rest = w_ref[:, 4 * d:]
    ba = rest[:, :n_ba].astype(BF16)
    ba_ref[...] = jnp.concatenate([ba, jnp.zeros((rows, HEAD_DIM - n_ba), BF16)], axis=1)
    sb_ref[...] = rest[:, n_ba:].astype(BF16)


def _regroup(w_in, *, d, n_ba, tr):
    depth, _, n_in = w_in.shape
    n_sb = n_in - 4 * d - n_ba
    out = lambda n: pl.BlockSpec((None, tr, n), lambda l, i: (l, i, 0))
    return pl.pallas_call(
        functools.partial(_regroup_body, d=d, n_ba=n_ba),
        grid=(depth, d // tr),
        in_specs=[pl.BlockSpec((None, tr, n_in), lambda l, i: (l, i, 0))],
        out_specs=[out(4 * d), out(n_sb), out(HEAD_DIM)],
        out_shape=[jax.ShapeDtypeStruct((depth, d, n), BF16) for n in (4 * d, n_sb, HEAD_DIM)],
        compiler_params=_params(("parallel", "parallel")),
        name="regroup",
    )(w_in)


def _row_tile(m, want):
    tm = min(want, m)
    assert m % tm == 0, (m, tm)
    return tm


def kernel(x, ffn1_norm, ffn1_w_in, ffn1_w_out, mix_norm, w_in, dn_conv_w, dn_a_log, dn_dt_bias,
           dn_out_norm, sb_q_norm, sb_k_norm, w_branch_a, w_branch_b, w_out,
           ffn2_norm, ffn2_w_in, ffn2_w_out):
    batch, seq, d = x.shape
    depth = ffn1_norm.shape[0]
    heads = d // HEAD_DIM
    assert seq % (2 * BLK) == 0 and d % HEAD_DIM == 0 and 2 * heads <= HEAD_DIM
    m = batch * seq
    tm_ffn = _row_tile(m, 512)
    tm_proj = _row_tile(m, 256)
    hg = 8 if heads % 8 == 0 else 1
    hs = 4 if heads % 4 == 0 else 1

    n_ba = 2 * heads
    w_dn, w_sb, w_ba = _regroup(w_in, d=d, n_ba=n_ba, tr=_row_tile(d, 256))
    lane_pad = lambda v: jnp.pad(v, ((0, 0), (heads, HEAD_DIM - n_ba)))[:, None, :]
    avec = lane_pad(-jnp.exp(dn_a_log.astype(F32)))
    dtvec = lane_pad(dn_dt_bias.astype(F32))
    f1_in, f1_out = ffn1_w_in.astype(BF16), ffn1_w_out.astype(BF16)
    f2_in, f2_out = ffn2_w_in.astype(BF16), ffn2_w_out.astype(BF16)
    wa, wb, wo = w_branch_a.astype(BF16), w_branch_b.astype(BF16), w_out.astype(BF16)

    xf = x.reshape(m, d)
    for l in range(depth):
        xf = _ffn(xf, ffn1_norm[l][None], f1_in, f1_out, l, tm=tm_ffn)
        dnqkv, z, qb, kb, vb, ga, gb, ba = _proj(
            xf, mix_norm[l][None], w_dn, w_sb, w_ba, l, sb_q_norm[l][None], sb_k_norm[l][None],
            avec[l], dtvec[l], tm=tm_proj, heads=heads)
        oa = _deltanet(dnqkv, dn_conv_w, l, ba, z, dn_out_norm[l][None],
                       batch=batch, seq=seq, heads=heads, hg=hg)
        ob = _stick_breaking(qb, kb, vb, batch=batch, seq=seq, heads=heads, hs=hs)
        xf = _merge(xf, oa, ob, ga, gb, wa, wb, wo, l, nq=seq // BLK)
        xf = _ffn(xf, ffn2_norm[l][None], f2_in, f2_out, l, tm=tm_ffn)
    return xf.reshape(batch, seq, d)
```

```python
import functools

import jax
import jax.numpy as jnp
from jax import lax
from jax.experimental import pallas as pl
from jax.experimental.pallas import tpu as pltpu

F32 = jnp.float32
BF16 = jnp.bfloat16

HEAD_DIM = 128
DN_CONV = 4
INV_BASE = 8
RMS_EPS = 1e-6
L2_EPS = 1e-6
NEG_BIG = -1e30
LOG2E = 1.4426950408889634

MXU_DIM = 256
BLK = MXU_DIM
VMEM_LIMIT_BYTES = 56 * 1024 * 1024


def _dot(a, b):
    return jnp.dot(a, b, preferred_element_type=F32)


def _dot_nt(a, b):
    return lax.dot_general(a, b, (((1,), (1,)), ((), ())), preferred_element_type=F32)


def _dot_tn(a, b):
    return lax.dot_general(a, b, (((0,), (0,)), ((), ())), preferred_element_type=F32)


def _softplus(x):
    return jnp.maximum(x, 0.0) + jnp.log1p(jnp.exp(-jnp.abs(x)))


def _softplus_fast(x):
    one = jnp.ones((), x.dtype)
    return jnp.maximum(x, 0 * one) + jnp.log(one + jnp.exp2(jnp.abs(x) * (-LOG2E * one)))


def _sigmoid(x):
    return 1.0 / (1.0 + jnp.exp(-x))


def _rms_rows(x, gain):
    ms = jnp.mean(x * x, axis=-1, keepdims=True)
    return x * lax.rsqrt(ms + RMS_EPS) * gain


def _split3(x):
    p0 = x.astype(BF16)
    r1 = x - p0.astype(F32)
    p1 = r1.astype(BF16)
    p2 = (r1 - p1.astype(F32)).astype(BF16)
    return p0, p1, p2


def _resident(shape):
    return pl.BlockSpec(shape, lambda *_: (0,) * len(shape), pipeline_mode=pl.Buffered(1))


def _resident_layer(stacked, layer):
    rest = stacked.shape[1:]
    return pl.BlockSpec((None,) + rest, lambda *_: (layer,) + (0,) * len(rest), pipeline_mode=pl.Buffered(1))


def _params(semantics):
    return pltpu.CompilerParams(dimension_semantics=semantics, vmem_limit_bytes=VMEM_LIMIT_BYTES)


def _ffn_body(x_ref, gain_ref, win_ref, wout_ref, o_ref, *, d_ff):
    x = x_ref[...]
    xn = _rms_rows(x, gain_ref[...]).astype(BF16)
    h = _dot(xn, win_ref[...])
    gate = h[:, :d_ff]
    up = h[:, d_ff:]
    act = (gate * _sigmoid(gate) * up).astype(BF16)
    o_ref[...] = x + 0.5 * _dot(act, wout_ref[...])


def _ffn(x, gain, w_in, w_out, layer, *, tm):
    m, d = x.shape
    d_ff = w_out.shape[1]
    return pl.pallas_call(
        functools.partial(_ffn_body, d_ff=d_ff),
        grid=(m // tm,),
        in_specs=[
            pl.BlockSpec((tm, d), lambda i: (i, 0)),
            _resident((1, d)),
            _resident_layer(w_in, layer),
            _resident_layer(w_out, layer),
        ],
        out_specs=pl.BlockSpec((tm, d), lambda i: (i, 0)),
        out_shape=jax.ShapeDtypeStruct((m, d), F32),
        compiler_params=_params(("parallel",)),
        name="ffn",
    )(x, gain, w_in, w_out)


def _proj_body(x_ref, gain_ref, wdn_ref, wsb_ref, wba_ref, qgain_ref, kgain_ref, avec_ref, dtvec_ref,
               dnqkv_ref, z_ref, qb_ref, kb_ref, vb_ref, ga_ref, gb_ref, ba_ref, *, d, heads):
    x = x_ref[...]
    xn = _rms_rows(x, gain_ref[...]).astype(BF16)
    dnqkv_ref[...] = _dot(xn, wdn_ref[:, 0:3 * d])
    z_ref[...] = _dot(xn, wdn_ref[:, 3 * d:4 * d]).astype(BF16)
    sq = _dot(xn, wsb_ref[:, 0:d])
    sk = _dot(xn, wsb_ref[:, d:2 * d])
    qgain = qgain_ref[...] * (HEAD_DIM ** -0.5)
    kgain = kgain_ref[...]
    for h in range(heads):
        sl = slice(h * HEAD_DIM, (h + 1) * HEAD_DIM)
        qb_ref[:, sl] = _rms_rows(sq[:, sl], qgain).astype(BF16)
        kb_ref[:, sl] = _rms_rows(sk[:, sl], kgain).astype(BF16)
    vb_ref[...] = _dot(xn, wsb_ref[:, 2 * d:3 * d]).astype(BF16)
    ga_ref[...] = _dot(xn, wsb_ref[:, 3 * d:4 * d]).astype(BF16)
    gb_ref[...] = _dot(xn, wsb_ref[:, 4 * d:5 * d]).astype(BF16)
    pba = _dot(xn, wba_ref[...])
    lane = lax.broadcasted_iota(jnp.int32, pba.shape, 1)
    beta = _sigmoid(pba)
    logdecay = avec_ref[...] * _softplus(pba + dtvec_ref[...])
    ba_ref[...] = jnp.where(lane < heads, beta, logdecay)


def _proj(x, gain, wdn, wsb, wba, layer, qgain, kgain, avec, dtvec, *, tm, heads):
    m, d = x.shape
    row = lambda i: (i, 0)
    f32_out = lambda n: jax.ShapeDtypeStruct((m, n), F32)
    bf_out = lambda n: jax.ShapeDtypeStruct((m, n), BF16)
    return pl.pallas_call(
        functools.partial(_proj_body, d=d, heads=heads),
        grid=(m // tm,),
        in_specs=[
            pl.BlockSpec((tm, d), row),
            _resident((1, d)),
            _resident_layer(wdn, layer),
            _resident_layer(wsb, layer),
            _resident_layer(wba, layer),
            _resident((1, HEAD_DIM)),
            _resident((1, HEAD_DIM)),
            _resident((1, HEAD_DIM)),
            _resident((1, HEAD_DIM)),
        ],
        out_specs=[
            pl.BlockSpec((tm, 3 * d), row),
            pl.BlockSpec((tm, d), row),
            pl.BlockSpec((tm, d), row),
            pl.BlockSpec((tm, d), row),
            pl.BlockSpec((tm, d), row),
            pl.BlockSpec((tm, d), row),
            pl.BlockSpec((tm, d), row),
            pl.BlockSpec((tm, HEAD_DIM), row),
        ],
        out_shape=[f32_out(3 * d), bf_out(d), bf_out(d), bf_out(d), bf_out(d),
                   bf_out(d), bf_out(d), f32_out(HEAD_DIM)],
        compiler_params=_params(("parallel",)),
        name="proj",
    )(x, gain, wdn, wsb, wba, qgain, kgain, avec, dtvec)


def _conv_silu(x_ref, w_ref, pre_ref, act_ref, idx, sl):
    rows = BLK // DN_CONV
    pre_ref[idx, 8:, :] = x_ref[:, sl]
    w = w_ref[:, sl]
    for r in range(DN_CONV):
        y = None
        for i in range(DN_CONV):
            window = pre_ref[idx, pl.ds(8 + r - i, rows, stride=DN_CONV), :]
            term = window * w[DN_CONV - 1 - i:DN_CONV - i, :]
            y = term if y is None else y + term
        act_ref[idx, pl.ds(r, rows, stride=DN_CONV), :] = y * _sigmoid(y)
    pre_ref[idx, 0:8, :] = x_ref[BLK - 8:, sl]


def _column(tile, lane_idx):
    lane = lax.broadcasted_iota(jnp.int32, tile.shape, 1)
    return jnp.sum(jnp.where(lane == lane_idx, tile, 0.0), axis=1, keepdims=True)


def _inverse_masks(row, col):
    shift = INV_BASE.bit_length() - 1
    blk = lambda x, s: lax.shift_right_logical(x, s)
    base = blk(row, shift) == blk(col, shift)
    joins = []
    while (1 << shift) < BLK:
        joins.append((blk(row, shift + 1) == blk(col, shift + 1)) & (blk(row, shift) != blk(col, shift)))
        shift += 1
    return base, joins


def _unit_lower_inverse(lmats, eye, masks):
    base, joins = masks
    blocks = [jnp.where(base, l, 0.0) for l in lmats]
    invs = [eye - b for b in blocks]
    powers = [b.astype(BF16) for b in blocks]
    span = 2
    while span < INV_BASE:
        powers = [_dot(p, p).astype(BF16) for p in powers]
        invs = [t + _dot(t.astype(BF16), p) for t, p in zip(invs, powers)]
        span *= 2
    l16s = [l.astype(BF16) for l in lmats]
    t16s = [t.astype(BF16) for t in invs]
    for join in joins[:-1]:
        pick = jnp.where(join, 1.0, 0.0).astype(BF16)
        cts = [_dot(l16 * pick, t16).astype(BF16) for l16, t16 in zip(l16s, t16s)]
        t16s = [t16 - _dot(t16, ct).astype(BF16) for t16, ct in zip(t16s, cts)]
    return t16s


def _unit_lower_solve(lmats, rhss, eye, masks):
    half = BLK // 2
    t16s = _unit_lower_inverse(lmats, eye, masks)
    tops = [_dot(t[:half, :half], r[:half].astype(BF16)) for t, r in zip(t16s, rhss)]
    lows = [r[half:] - _dot(l[half:, :half].astype(BF16), top.astype(BF16)) for l, r, top in zip(lmats, rhss, tops)]
    bottoms = [_dot(t[half:, half:], low.astype(BF16)) for t, low in zip(t16s, lows)]
    return [jnp.concatenate([top, bottom], axis=0) for top, bottom in zip(tops, bottoms)]


def _dn_body(xq_ref, xk_ref, xv_ref, wq_ref, wk_ref, wv_ref, ba_ref, z_ref, ogain_ref,
             o_ref, state_ref, pre_ref, act_ref, gct_ref, *, hg, heads):
    t = pl.program_id(2)
    group = pl.program_id(1)
    js = range(hg)
    lanes = [slice(j * HEAD_DIM, (j + 1) * HEAD_DIM) for j in js]

    @pl.when(t == 0)
    def _():
        state_ref[...] = jnp.zeros_like(state_ref)
        pre_ref[:, 0:8, :] = jnp.zeros((3 * hg, 8, HEAD_DIM), F32)

    for part, (x_ref, w_ref) in enumerate(((xq_ref, wq_ref), (xk_ref, wk_ref), (xv_ref, wv_ref))):
        for j in js:
            _conv_silu(x_ref, w_ref, pre_ref, act_ref, part * hg + j, lanes[j])

    row = lax.broadcasted_iota(jnp.int32, (BLK, BLK), 0)
    col = lax.broadcasted_iota(jnp.int32, (BLK, BLK), 1)
    lower_incl = row >= col
    strict = row > col
    eye = jnp.where(row == col, 1.0, 0.0)
    masks = _inverse_masks(row, col)
    cum_mat = jnp.where(lower_incl, 1.0, 0.0).astype(BF16)

    ba = ba_ref[...]
    g0, g1, g2 = _split3(ba)
    gc_all = _dot(cum_mat, g0) + _dot(cum_mat, g1) + _dot(cum_mat, g2)
    gl_all = gc_all[BLK - 1:BLK, :]
    gct_ref[...] = gc_all.T
    egc_all = jnp.exp(gc_all)
    ekd_all = jnp.exp(gl_all - gc_all)
    egl_all = jnp.broadcast_to(jnp.exp(gl_all), (HEAD_DIM, HEAD_DIM))

    ogain = ogain_ref[...]
    hs = [group * hg + j for j in js]
    l2 = lambda x: lax.rsqrt(jnp.sum(x * x, axis=-1, keepdims=True) + L2_EPS)
    qs = [act_ref[j] * (l2(act_ref[j]) * (HEAD_DIM ** -0.5)) for j in js]
    ks = [act_ref[hg + j] * l2(act_ref[hg + j]) for j in js]
    vs = [act_ref[2 * hg + j] for j in js]
    betas = [_column(ba, h) for h in hs]
    egcs = [_column(egc_all, heads + h) for h in hs]
    ekds = [_column(ekd_all, heads + h) for h in hs]
    egls = [_column(egl_all, heads + h) for h in hs]
    decays = [jnp.exp(jnp.where(lower_incl,
                                _column(gc_all, heads + h) - gct_ref[pl.ds(heads + h, 1), :], NEG_BIG))
              for h in hs]
    kbs = [k * b for k, b in zip(ks, betas)]
    k16s = [k.astype(BF16) for k in ks]
    lmats = [jnp.where(strict, _dot_nt(kb.astype(BF16), k16) * dec, 0.0)
             for kb, k16, dec in zip(kbs, k16s, decays)]
    attn16s = [(_dot_nt(q.astype(BF16), k16) * dec).astype(BF16) for q, k16, dec in zip(qs, k16s, decays)]
    rhss = [jnp.concatenate([v * b, kb * egc], axis=1)
            for v, b, kb, egc in zip(vs, betas, kbs, egcs)]
    sols = _unit_lower_solve(lmats, rhss, eye, masks)
    qd16s = [(q * egc).astype(BF16) for q, egc in zip(qs, egcs)]
    kd16s = [(k * ekd).astype(BF16) for k, ekd in zip(ks, ekds)]
    states = [state_ref[j] for j in js]
    stacked = [jnp.concatenate([sol[:, HEAD_DIM:].astype(BF16), qd16], axis=0)
               for sol, qd16 in zip(sols, qd16s)]
    s16s = [s.astype(BF16) for s in states]
    if hg % 2 == 0:
        zero = jnp.zeros((HEAD_DIM, HEAD_DIM), BF16)
        ws_qss = []
        for a in range(0, hg, 2):
            both = _dot(jnp.concatenate([stacked[a], stacked[a + 1]], axis=1),
                        jnp.concatenate([jnp.concatenate([s16s[a], zero], axis=1),
                                         jnp.concatenate([zero, s16s[a + 1]], axis=1)], axis=0))
            ws_qss += [both[:, :HEAD_DIM], both[:, HEAD_DIM:]]
    else:
        ws_qss = [_dot(lhs, s16) for lhs, s16 in zip(stacked, s16s)]
    vnew16s = [(sol[:, :HEAD_DIM] - ws_qs[:BLK]).astype(BF16) for sol, ws_qs in zip(sols, ws_qss)]
    outs = [ws_qs[BLK:] + _dot(attn16, vnew16) for ws_qs, attn16, vnew16 in zip(ws_qss, attn16s, vnew16s)]
    for j, s, egl, kd16, vnew16 in zip(js, states, egls, kd16s, vnew16s):
        state_ref[j] = s * egl + _dot_tn(kd16, vnew16)
    for sl, o in zip(lanes, outs):
        zt = z_ref[:, sl].astype(F32)
        o_ref[:, sl] = (_rms_rows(o, ogain) * (zt * _sigmoid(zt))).astype(BF16)


def _deltanet(dnqkv, conv_w, layer, ba, z, ogain, *, batch, seq, heads, hg):
    m = dnqkv.shape[0]
    d = heads * HEAD_DIM
    nt = seq // BLK
    ng = heads // hg
    wide = hg * HEAD_DIM
    rows = lambda b, g, t: b * nt + t
    x_part = lambda part: pl.BlockSpec((BLK, wide), lambda b, g, t: (rows(b, g, t), part * ng + g))
    w_part = lambda part: pl.BlockSpec((None, DN_CONV, wide), lambda b, g, t: (layer, 0, part * ng + g))
    return pl.pallas_call(
        functools.partial(_dn_body, hg=hg, heads=heads),
        grid=(batch, ng, nt),
        in_specs=[
            x_part(0), x_part(1), x_part(2),
            w_part(0), w_part(1), w_part(2),
            pl.BlockSpec((BLK, HEAD_DIM), lambda b, g, t: (rows(b, g, t), 0)),
            pl.BlockSpec((BLK, wide), lambda b, g, t: (rows(b, g, t), g)),
            pl.BlockSpec((1, HEAD_DIM), lambda b, g, t: (0, 0)),
        ],
        out_specs=pl.BlockSpec((BLK, wide), lambda b, g, t: (rows(b, g, t), g)),
        out_shape=jax.ShapeDtypeStruct((m, d), BF16),
        scratch_shapes=[
            pltpu.VMEM((hg, HEAD_DIM, HEAD_DIM), F32),
            pltpu.VMEM((3 * hg, 8 + BLK, HEAD_DIM), F32),
            pltpu.VMEM((3 * hg, BLK, HEAD_DIM), F32),
            pltpu.VMEM((HEAD_DIM, BLK), F32),
        ],
        compiler_params=_params(("parallel", "parallel", "arbitrary")),
        name="deltanet",
    )(dnqkv, dnqkv, dnqkv, conv_w, conv_w, conv_w, ba, z, ogain)


def _sb_body(qa_ref, qb_ref, k_ref, v_ref, o_ref, qs_ref, acc_ref, right_ref, *, hs, nq):
    p = pl.program_id(2)
    long_blk = nq - 1 - p
    row = lax.broadcasted_iota(jnp.int32, (BLK, BLK), 0)
    col = lax.broadcasted_iota(jnp.int32, (BLK, BLK), 1)
    suffix = jnp.where(row >= col, 1.0, 0.0).astype(BF16)
    causal = col < row
    lanes = [slice(h * HEAD_DIM, (h + 1) * HEAD_DIM) for h in range(hs)]
    qs_ref[0] = qa_ref[...]
    qs_ref[1] = qb_ref[...]

    tiles = [(0, long_blk, True), (1, p, True)]
    for m in range(nq - 1):
        in_short = m >= long_blk
        tiles.append((jnp.where(in_short, 1, 0), jnp.where(in_short, p - 1 - (m - long_blk), long_blk - 1 - m), False))

    def scores(slot, kblk):
        start = pl.multiple_of(kblk * BLK, BLK)
        return [_dot_nt(qs_ref[slot, :, sl], k_ref[pl.ds(start, BLK), sl]) for sl in lanes]

    def cumsums(zs, diagonal):
        sps = [_softplus_fast(z.astype(BF16)) for z in zs]
        if diagonal:
            sps = [jnp.where(causal, sp, jnp.zeros_like(sp)) for sp in sps]
        return [_dot(sp, suffix) for sp in sps]

    def accumulate(slot, kblk, diagonal, zs, cums):
        start = pl.multiple_of(kblk * BLK, BLK)
        if diagonal:
            ws = [jnp.where(causal, jnp.exp(z - cum), 0.0) for z, cum in zip(zs, cums)]
        else:
            rights = [right_ref[slot, h] for h in range(hs)]
            ws = [jnp.exp(z - cum - jnp.concatenate([r, r], axis=1)) for z, cum, r in zip(zs, cums, rights)]
        pvs = [_dot(w.astype(BF16), v_ref[pl.ds(start, BLK), sl]) for w, sl in zip(ws, lanes)]
        totals = [jnp.broadcast_to(cum[:, 0:1], (BLK, HEAD_DIM)) for cum in cums]
        for h in range(hs):
            if diagonal:
                acc_ref[slot, h] = pvs[h]
                right_ref[slot, h] = totals[h]
            else:
                acc_ref[slot, h] = acc_ref[slot, h] + pvs[h]
                right_ref[slot, h] = rights[h] + totals[h]

    n_tiles = len(tiles)
    zs = {0: scores(*tiles[0][:2])}
    for n in range(n_tiles):
        if n + 1 < n_tiles:
            zs[n + 1] = scores(*tiles[n + 1][:2])
        z = zs.pop(n)
        accumulate(*tiles[n], z, cumsums(z, tiles[n][2]))

    for slot in range(2):
        for h, sl in enumerate(lanes):
            o_ref[slot, :, sl] = acc_ref[slot, h].astype(BF16)


def _stick_breaking(qb, kb, vb, *, batch, seq, heads, hs):
    m, d = qb.shape
    nq = seq // BLK
    assert nq % 2 == 0
    wide = hs * HEAD_DIM
    return pl.pallas_call(
        functools.partial(_sb_body, hs=hs, nq=nq),
        grid=(batch, heads // hs, nq // 2),
        in_specs=[
            pl.BlockSpec((BLK, wide), lambda b, h, p: (b * nq + (nq - 1 - p), h)),
            pl.BlockSpec((BLK, wide), lambda b, h, p: (b * nq + p, h)),
            pl.BlockSpec((seq, wide), lambda b, h, p: (b, h)),
            pl.BlockSpec((seq, wide), lambda b, h, p: (b, h)),
        ],
        out_specs=pl.BlockSpec((None, None, 2, BLK, wide), lambda b, h, p: (b, p, 0, 0, h)),
        out_shape=jax.ShapeDtypeStruct((batch, nq // 2, 2, BLK, d), BF16),
        scratch_shapes=[
            pltpu.VMEM((2, BLK, wide), BF16),
            pltpu.VMEM((2, hs, BLK, HEAD_DIM), F32),
            pltpu.VMEM((2, hs, BLK, HEAD_DIM), F32),
        ],
        compiler_params=_params(("parallel", "parallel", "arbitrary")),
        name="stickbrk",
    )(qb, qb, kb, vb)


def _merge_body(x_ref, oa_ref, ob0_ref, ob1_ref, ga_ref, gb_ref, wa_ref, wb_ref, wo_ref, o_ref):
    ob = jnp.concatenate([ob0_ref[...], ob1_ref[...]], axis=0)
    ya = _dot(oa_ref[...], wa_ref[...])
    yb = _dot(ob, wb_ref[...])
    merged = _sigmoid(ga_ref[...].astype(F32)) * ya + _sigmoid(gb_ref[...].astype(F32)) * yb
    o_ref[...] = x_ref[...] + _dot(merged.astype(BF16), wo_ref[...])


def _merge(x, oa, ob, ga, gb, wa, wb, wo, layer, *, nq):
    m, d = x.shape
    tile = pl.BlockSpec((2 * BLK, d), lambda i: (i, 0))

    def ob_block(which):
        def index(i):
            blk = lax.rem(2 * i + which, nq)
            return (lax.div(2 * i + which, nq), jnp.minimum(blk, nq - 1 - blk), jnp.where(blk < nq // 2, 1, 0), 0, 0)
        return pl.BlockSpec((None, None, None, BLK, d), index)

    return pl.pallas_call(
        _merge_body,
        grid=(m // (2 * BLK),),
        in_specs=[tile, tile, ob_block(0), ob_block(1), tile, tile,
                  _resident_layer(wa, layer), _resident_layer(wb, layer), _resident_layer(wo, layer)],
        out_specs=tile,
        out_shape=jax.ShapeDtypeStruct((m, d), F32),
        compiler_params=_params(("parallel",)),
        name="merge",
    )(x, oa, ob, ob, ga, gb, wa, wb, wo)


def _row_tile(m, want):
    tm = min(want, m)
    assert m % tm == 0, (m, tm)
    return tm


def kernel(x, ffn1_norm, ffn1_w_in, ffn1_w_out, mix_norm, w_in, dn_conv_w, dn_a_log, dn_dt_bias,
           dn_out_norm, sb_q_norm, sb_k_norm, w_branch_a, w_branch_b, w_out,
           ffn2_norm, ffn2_w_in, ffn2_w_out):
    batch, seq, d = x.shape
    depth = ffn1_norm.shape[0]
    heads = d // HEAD_DIM
    assert seq % (2 * BLK) == 0 and d % HEAD_DIM == 0 and 2 * heads <= HEAD_DIM
    m = batch * seq
    tm_ffn = _row_tile(m, 512)
    tm_proj = _row_tile(m, 512)
    hg = 8 if heads % 8 == 0 else 1
    hs = 4 if heads % 4 == 0 else 1

    n_ba = 2 * heads
    w_dn = w_in[:, :, :4 * d].astype(BF16)
    w_sb = w_in[:, :, 4 * d + n_ba:].astype(BF16)
    w_ba = jnp.pad(w_in[:, :, 4 * d:4 * d + n_ba], ((0, 0), (0, 0), (0, HEAD_DIM - n_ba))).astype(BF16)
    lane_pad = lambda v: jnp.pad(v, ((0, 0), (heads, HEAD_DIM - n_ba)))[:, None, :]
    avec = lane_pad(-jnp.exp(dn_a_log.astype(F32)))
    dtvec = lane_pad(dn_dt_bias.astype(F32))
    f1_in, f1_out = ffn1_w_in.astype(BF16), ffn1_w_out.astype(BF16)
    f2_in, f2_out = ffn2_w_in.astype(BF16), ffn2_w_out.astype(BF16)
    wa, wb, wo = w_branch_a.astype(BF16), w_branch_b.astype(BF16), w_out.astype(BF16)

    xf = x.reshape(m, d)
    for l in range(depth):
        xf = _ffn(xf, ffn1_norm[l][None], f1_in, f1_out, l, tm=tm_ffn)
        dnqkv, z, qb, kb, vb, ga, gb, ba = _proj(
            xf, mix_norm[l][None], w_dn, w_sb, w_ba, l, sb_q_norm[l][None], sb_k_norm[l][None],
            avec[l], dtvec[l], tm=tm_proj, heads=heads)
        oa = _deltanet(dnqkv, dn_conv_w, l, ba, z, dn_out_norm[l][None],
                       batch=batch, seq=seq, heads=heads, hg=hg)
        ob = _stick_breaking(qb, kb, vb, batch=batch, seq=seq, heads=heads, hs=hs)
        xf = _merge(xf, oa, ob, ga, gb, wa, wb, wo, l, nq=seq // BLK)
        xf = _ffn(xf, ffn2_norm[l][None], f2_in, f2_out, l, tm=tm_ffn)
    return xf.reshape(batch, seq, d)
```

```python
import functools

import jax
import jax.numpy as jnp
from jax import lax
from jax.experimental import pallas as pl
from jax.experimental.pallas import tpu as pltpu

F32 = jnp.float32
BF16 = jnp.bfloat16

HEAD_DIM = 128
DN_CONV = 4
INV_BASE = 8
RMS_EPS = 1e-6
L2_EPS = 1e-6
NEG_BIG = -1e30
LOG2E = 1.4426950408889634

MXU_DIM = 256
BLK = MXU_DIM
VMEM_LIMIT_BYTES = 56 * 1024 * 1024


def _dot(a, b):
    return jnp.dot(a, b, preferred_element_type=F32)


def _dot_nt(a, b):
    return lax.dot_general(a, b, (((1,), (1,)), ((), ())), preferred_element_type=F32)


def _dot_tn(a, b):
    return lax.dot_general(a, b, (((0,), (0,)), ((), ())), preferred_element_type=F32)


def _softplus(x):
    return jnp.maximum(x, 0.0) + jnp.log1p(jnp.exp(-jnp.abs(x)))


def _softplus_fast(x):
    one = jnp.ones((), x.dtype)
    return jnp.maximum(x, 0 * one) + jnp.log(one + jnp.exp2(jnp.abs(x) * (-LOG2E * one)))


def _sigmoid(x):
    return 1.0 / (1.0 + jnp.exp(-x))


def _rms_rows(x, gain):
    ms = jnp.mean(x * x, axis=-1, keepdims=True)
    return x * lax.rsqrt(ms + RMS_EPS) * gain


def _split3(x):
    p0 = x.astype(BF16)
    r1 = x - p0.astype(F32)
    p1 = r1.astype(BF16)
    p2 = (r1 - p1.astype(F32)).astype(BF16)
    return p0, p1, p2


def _resident(shape):
    return pl.BlockSpec(shape, lambda *_: (0,) * len(shape), pipeline_mode=pl.Buffered(1))


def _resident_layer(stacked, layer):
    rest = stacked.shape[1:]
    return pl.BlockSpec((None,) + rest, lambda *_: (layer,) + (0,) * len(rest), pipeline_mode=pl.Buffered(1))


def _params(semantics):
    return pltpu.CompilerParams(dimension_semantics=semantics, vmem_limit_bytes=VMEM_LIMIT_BYTES)


def _ffn_body(x_ref, gain_ref, win_ref, wout_ref, o_ref, *, d_ff):
    x = x_ref[...]
    xn = _rms_rows(x, gain_ref[...]).astype(BF16)
    h = _dot(xn, win_ref[...])
    gate = h[:, :d_ff]
    up = h[:, d_ff:]
    act = (gate * _sigmoid(gate) * up).astype(BF16)
    o_ref[...] = x + 0.5 * _dot(act, wout_ref[...])


def _ffn(x, gain, w_in, w_out, layer, *, tm):
    m, d = x.shape
    d_ff = w_out.shape[1]
    return pl.pallas_call(
        functools.partial(_ffn_body, d_ff=d_ff),
        grid=(m // tm,),
        in_specs=[
            pl.BlockSpec((tm, d), lambda i: (i, 0)),
            _resident((1, d)),
            _resident_layer(w_in, layer),
            _resident_layer(w_out, layer),
        ],
        out_specs=pl.BlockSpec((tm, d), lambda i: (i, 0)),
        out_shape=jax.ShapeDtypeStruct((m, d), F32),
        compiler_params=_params(("parallel",)),
        name="ffn",
    )(x, gain, w_in, w_out)


def _proj_body(x_ref, gain_ref, wdn_ref, wsb_ref, wba_ref, qgain_ref, kgain_ref, avec_ref, dtvec_ref,
               dnqkv_ref, z_ref, qb_ref, kb_ref, vb_ref, ga_ref, gb_ref, ba_ref, *, d, heads):
    x = x_ref[...]
    xn = _rms_rows(x, gain_ref[...]).astype(BF16)
    dnqkv_ref[...] = _dot(xn, wdn_ref[:, 0:3 * d])
    z_ref[...] = _dot(xn, wdn_ref[:, 3 * d:4 * d]).astype(BF16)
    sq = _dot(xn, wsb_ref[:, 0:d])
    sk = _dot(xn, wsb_ref[:, d:2 * d])
    qgain = qgain_ref[...] * (HEAD_DIM ** -0.5)
    kgain = kgain_ref[...]
    for h in range(heads):
        sl = slice(h * HEAD_DIM, (h + 1) * HEAD_DIM)
        qb_ref[:, sl] = _rms_rows(sq[:, sl], qgain).astype(BF16)
        kb_ref[:, sl] = _rms_rows(sk[:, sl], kgain).astype(BF16)
    vb_ref[...] = _dot(xn, wsb_ref[:, 2 * d:3 * d]).astype(BF16)
    ga_ref[...] = _dot(xn, wsb_ref[:, 3 * d:4 * d]).astype(BF16)
    gb_ref[...] = _dot(xn, wsb_ref[:, 4 * d:5 * d]).astype(BF16)
    pba = _dot(xn, wba_ref[...])
    lane = lax.broadcasted_iota(jnp.int32, pba.shape, 1)
    beta = _sigmoid(pba)
    logdecay = avec_ref[...] * _softplus(pba + dtvec_ref[...])
    ba_ref[...] = jnp.where(lane < heads, beta, logdecay)


def _proj(x, gain, wdn, wsb, wba, layer, qgain, kgain, avec, dtvec, *, tm, heads):
    m, d = x.shape
    row = lambda i: (i, 0)
    f32_out = lambda n: jax.ShapeDtypeStruct((m, n), F32)
    bf_out = lambda n: jax.ShapeDtypeStruct((m, n), BF16)
    return pl.pallas_call(
        functools.partial(_proj_body, d=d, heads=heads),
        grid=(m // tm,),
        in_specs=[
            pl.BlockSpec((tm, d), row),
            _resident((1, d)),
            _resident_layer(wdn, layer),
            _resident_layer(wsb, layer),
            _resident_layer(wba, layer),
            _resident((1, HEAD_DIM)),
            _resident((1, HEAD_DIM)),
            _resident((1, HEAD_DIM)),
            _resident((1, HEAD_DIM)),
        ],
        out_specs=[
            pl.BlockSpec((tm, 3 * d), row),
            pl.BlockSpec((tm, d), row),
            pl.BlockSpec((tm, d), row),
            pl.BlockSpec((tm, d), row),
            pl.BlockSpec((tm, d), row),
            pl.BlockSpec((tm, d), row),
            pl.BlockSpec((tm, d), row),
            pl.BlockSpec((tm, HEAD_DIM), row),
        ],
        out_shape=[f32_out(3 * d), bf_out(d), bf_out(d), bf_out(d), bf_out(d),
                   bf_out(d), bf_out(d), f32_out(HEAD_DIM)],
        compiler_params=_params(("parallel",)),
        name="proj",
    )(x, gain, wdn, wsb, wba, qgain, kgain, avec, dtvec)


def _conv_silu(x_ref, w_ref, pre_ref, act_ref, idx, sl):
    rows = BLK // DN_CONV
    pre_ref[idx, 8:, :] = x_ref[:, sl]
    w = w_ref[:, sl]
    for r in range(DN_CONV):
        y = None
        for i in range(DN_CONV):
            window = pre_ref[idx, pl.ds(8 + r - i, rows, stride=DN_CONV), :]
            term = window * w[DN_CONV - 1 - i:DN_CONV - i, :]
            y = term if y is None else y + term
        act_ref[idx, pl.ds(r, rows, stride=DN_CONV), :] = y * _sigmoid(y)
    pre_ref[idx, 0:8, :] = x_ref[BLK - 8:, sl]


def _column(tile, lane_idx):
    lane = lax.broadcasted_iota(jnp.int32, tile.shape, 1)
    return jnp.sum(jnp.where(lane == lane_idx, tile, 0.0), axis=1, keepdims=True)


def _inverse_masks(row, col):
    shift = INV_BASE.bit_length() - 1
    blk = lambda x, s: lax.shift_right_logical(x, s)
    base = blk(row, shift) == blk(col, shift)
    joins = []
    while (1 << shift) < BLK:
        joins.append((blk(row, shift + 1) == blk(col, shift + 1)) & (blk(row, shift) != blk(col, shift)))
        shift += 1
    return base, joins


def _unit_lower_inverse(lmats, eye, masks):
    base, joins = masks
    blocks = [jnp.where(base, l, 0.0) for l in lmats]
    invs = [eye - b for b in blocks]
    powers = [b.astype(BF16) for b in blocks]
    span = 2
    while span < INV_BASE:
        powers = [_dot(p, p).astype(BF16) for p in powers]
        invs = [t + _dot(t.astype(BF16), p) for t, p in zip(invs, powers)]
        span *= 2
    l16s = [l.astype(BF16) for l in lmats]
    t16s = [t.astype(BF16) for t in invs]
    for join in joins[:-1]:
        pick = jnp.where(join, 1.0, 0.0).astype(BF16)
        cts = [_dot(l16 * pick, t16).astype(BF16) for l16, t16 in zip(l16s, t16s)]
        t16s = [t16 - _dot(t16, ct).astype(BF16) for t16, ct in zip(t16s, cts)]
    return t16s


def _unit_lower_solve(lmats, rhss, eye, masks):
    half = BLK // 2
    t16s = _unit_lower_inverse(lmats, eye, masks)
    tops = [_dot(t[:half, :half], r[:half].astype(BF16)) for t, r in zip(t16s, rhss)]
    lows = [r[half:] - _dot(l[half:, :half].astype(BF16), top.astype(BF16)) for l, r, top in zip(lmats, rhss, tops)]
    bottoms = [_dot(t[half:, half:], low.astype(BF16)) for t, low in zip(t16s, lows)]
    return [jnp.concatenate([top, bottom], axis=0) for top, bottom in zip(tops, bottoms)]


def _dn_body(xq_ref, xk_ref, xv_ref, wq_ref, wk_ref, wv_ref, ba_ref, z_ref, ogain_ref,
             o_ref, state_ref, pre_ref, act_ref, gct_ref, *, hg, heads):
    t = pl.program_id(2)
    group = pl.program_id(1)
    js = range(hg)
    lanes = [slice(j * HEAD_DIM, (j + 1) * HEAD_DIM) for j in js]

    @pl.when(t == 0)
    def _():
        state_ref[...] = jnp.zeros_like(state_ref)
        pre_ref[:, 0:8, :] = jnp.zeros((3 * hg, 8, HEAD_DIM), F32)

    for part, (x_ref, w_ref) in enumerate(((xq_ref, wq_ref), (xk_ref, wk_ref), (xv_ref, wv_ref))):
        for j in js:
            _conv_silu(x_ref, w_ref, pre_ref, act_ref, part * hg + j, lanes[j])

    row = lax.broadcasted_iota(jnp.int32, (BLK, BLK), 0)
    col = lax.broadcasted_iota(jnp.int32, (BLK, BLK), 1)
    lower_incl = row >= col
    strict = row > col
    eye = jnp.where(row == col, 1.0, 0.0)
    masks = _inverse_masks(row, col)
    cum_mat = jnp.where(lower_incl, 1.0, 0.0).astype(BF16)

    ba = ba_ref[...]
    g0, g1, g2 = _split3(ba)
    gc_all = _dot(cum_mat, g0) + _dot(cum_mat, g1) + _dot(cum_mat, g2)
    gl_all = gc_all[BLK - 1:BLK, :]
    gct_ref[...] = gc_all.T
    egc_all = jnp.exp(gc_all)
    ekd_all = jnp.exp(gl_all - gc_all)
    egl_all = jnp.broadcast_to(jnp.exp(gl_all), (HEAD_DIM, HEAD_DIM))

    ogain = ogain_ref[...]
    hs = [group * hg + j for j in js]
    l2 = lambda x: lax.rsqrt(jnp.sum(x * x, axis=-1, keepdims=True) + L2_EPS)
    qs = [act_ref[j] * (l2(act_ref[j]) * (HEAD_DIM ** -0.5)) for j in js]
    ks = [act_ref[hg + j] * l2(act_ref[hg + j]) for j in js]
    vs = [act_ref[2 * hg + j] for j in js]
    betas = [_column(ba, h) for h in hs]
    egcs = [_column(egc_all, heads + h) for h in hs]
    ekds = [_column(ekd_all, heads + h) for h in hs]
    egls = [_column(egl_all, heads + h) for h in hs]
    decays = [jnp.exp(jnp.where(lower_incl,
                                _column(gc_all, heads + h) - gct_ref[pl.ds(heads + h, 1), :], NEG_BIG))
              for h in hs]
    kbs = [k * b for k, b in zip(ks, betas)]
    k16s = [k.astype(BF16) for k in ks]
    lmats = [jnp.where(strict, _dot_nt(kb.astype(BF16), k16) * dec, 0.0)
             for kb, k16, dec in zip(kbs, k16s, decays)]
    attn16s = [(_dot_nt(q.astype(BF16), k16) * dec).astype(BF16) for q, k16, dec in zip(qs, k16s, decays)]
    rhss = [jnp.concatenate([v * b, kb * egc], axis=1)
            for v, b, kb, egc in zip(vs, betas, kbs, egcs)]
    sols = _unit_lower_solve(lmats, rhss, eye, masks)
    qd16s = [(q * egc).astype(BF16) for q, egc in zip(qs, egcs)]
    kd16s = [(k * ekd).astype(BF16) for k, ekd in zip(ks, ekds)]
    states = [state_ref[j] for j in js]
    stacked = [jnp.concatenate([sol[:, HEAD_DIM:].astype(BF16), qd16], axis=0)
               for sol, qd16 in zip(sols, qd16s)]
    s16s = [s.astype(BF16) for s in states]
    if hg % 2 == 0:
        zero = jnp.zeros((HEAD_DIM, HEAD_DIM), BF16)
        ws_qss = []
        for a in range(0, hg, 2):
            both = _dot(jnp.concatenate([stacked[a], stacked[a + 1]], axis=1),
                        jnp.concatenate([jnp.concatenate([s16s[a], zero], axis=1),
                                         jnp.concatenate([zero, s16s[a + 1]], axis=1)], axis=0))
            ws_qss += [both[:, :HEAD_DIM], both[:, HEAD_DIM:]]
    else:
        ws_qss = [_dot(lhs, s16) for lhs, s16 in zip(stacked, s16s)]
    vnew16s = [(sol[:, :HEAD_DIM] - ws_qs[:BLK]).astype(BF16) for sol, ws_qs in zip(sols, ws_qss)]
    outs = [ws_qs[BLK:] + _dot(attn16, vnew16) for ws_qs, attn16, vnew16 in zip(ws_qss, attn16s, vnew16s)]
    for j, s, egl, kd16, vnew16 in zip(js, states, egls, kd16s, vnew16s):
        state_ref[j] = s * egl + _dot_tn(kd16, vnew16)
    for sl, o in zip(lanes, outs):
        zt = z_ref[:, sl].astype(F32)
        o_ref[:, sl] = (_rms_rows(o, ogain) * (zt * _sigmoid(zt))).astype(BF16)


def _deltanet(dnqkv, conv_w, layer, ba, z, ogain, *, batch, seq, heads, hg):
    m = dnqkv.shape[0]
    d = heads * HEAD_DIM
    nt = seq // BLK
    ng = heads // hg
    wide = hg * HEAD_DIM
    rows = lambda b, g, t: b * nt + t
    x_part = lambda part: pl.BlockSpec((BLK, wide), lambda b, g, t: (rows(b, g, t), part * ng + g))
    w_part = lambda part: pl.BlockSpec((None, DN_CONV, wide), lambda b, g, t: (layer, 0, part * ng + g))
    return pl.pallas_call(
        functools.partial(_dn_body, hg=hg, heads=heads),
        grid=(batch, ng, nt),
        in_specs=[
            x_part(0), x_part(1), x_part(2),
            w_part(0), w_part(1), w_part(2),
            pl.BlockSpec((BLK, HEAD_DIM), lambda b, g, t: (rows(b, g, t), 0)),
            pl.BlockSpec((BLK, wide), lambda b, g, t: (rows(b, g, t), g)),
            pl.BlockSpec((1, HEAD_DIM), lambda b, g, t: (0, 0)),
        ],
        out_specs=pl.BlockSpec((BLK, wide), lambda b, g, t: (rows(b, g, t), g)),
        out_shape=jax.ShapeDtypeStruct((m, d), BF16),
        scratch_shapes=[
            pltpu.VMEM((hg, HEAD_DIM, HEAD_DIM), F32),
            pltpu.VMEM((3 * hg, 8 + BLK, HEAD_DIM), F32),
            pltpu.VMEM((3 * hg, BLK, HEAD_DIM), F32),
            pltpu.VMEM((HEAD_DIM, BLK), F32),
        ],
        compiler_params=_params(("parallel", "parallel", "arbitrary")),
        name="deltanet",
    )(dnqkv, dnqkv, dnqkv, conv_w, conv_w, conv_w, ba, z, ogain)


def _sb_body(qa_ref, qb_ref, k_ref, v_ref, o_ref, qs_ref, acc_ref, right_ref, *, hs, nq):
    p = pl.program_id(2)
    long_blk = nq - 1 - p
    row = lax.broadcasted_iota(jnp.int32, (BLK, BLK), 0)
    col = lax.broadcasted_iota(jnp.int32, (BLK, BLK), 1)
    suffix = jnp.where(row >= col, 1.0, 0.0).astype(BF16)
    causal = col < row
    lanes = [slice(h * HEAD_DIM, (h + 1) * HEAD_DIM) for h in range(hs)]
    qs_ref[0] = qa_ref[...]
    qs_ref[1] = qb_ref[...]

    tiles = [(0, long_blk, True), (1, p, True)]
    for m in range(nq - 1):
        in_short = m >= long_blk
        tiles.append((jnp.where(in_short, 1, 0), jnp.where(in_short, p - 1 - (m - long_blk), long_blk - 1 - m), False))

    def scores(slot, kblk):
        start = pl.multiple_of(kblk * BLK, BLK)
        return [_dot_nt(qs_ref[slot, :, sl], k_ref[pl.ds(start, BLK), sl]) for sl in lanes]

    def cumsums(zs, diagonal):
        sps = [_softplus_fast(z.astype(BF16)) for z in zs]
        if diagonal:
            sps = [jnp.where(causal, sp, jnp.zeros_like(sp)) for sp in sps]
        return [_dot(sp, suffix) for sp in sps]

    def accumulate(slot, kblk, diagonal, zs, cums):
        start = pl.multiple_of(kblk * BLK, BLK)
        if diagonal:
            ws = [jnp.where(causal, jnp.exp(z - cum), 0.0) for z, cum in zip(zs, cums)]
        else:
            rights = [right_ref[slot, h] for h in range(hs)]
            ws = [jnp.exp(z - cum - jnp.concatenate([r, r], axis=1)) for z, cum, r in zip(zs, cums, rights)]
        pvs = [_dot(w.astype(BF16), v_ref[pl.ds(start, BLK), sl]) for w, sl in zip(ws, lanes)]
        totals = [jnp.broadcast_to(cum[:, 0:1], (BLK, HEAD_DIM)) for cum in cums]
        for h in range(hs):
            if diagonal:
                acc_ref[slot, h] = pvs[h]
                right_ref[slot, h] = totals[h]
            else:
                acc_ref[slot, h] = acc_ref[slot, h] + pvs[h]
                right_ref[slot, h] = rights[h] + totals[h]

    n_tiles = len(tiles)
    zs = {0: scores(*tiles[0][:2])}
    for n in range(n_tiles):
        if n + 1 < n_tiles:
            zs[n + 1] = scores(*tiles[n + 1][:2])
        z = zs.pop(n)
        accumulate(*tiles[n], z, cumsums(z, tiles[n][2]))

    for slot in range(2):
        for h, sl in enumerate(lanes):
            o_ref[slot, :, sl] = acc_ref[slot, h].astype(BF16)


def _stick_breaking(qb, kb, vb, *, batch, seq, heads, hs):
    m, d = qb.shape
    nq = seq // BLK
    assert nq % 2 == 0
    wide = hs * HEAD_DIM
    return pl.pallas_call(
        functools.partial(_sb_body, hs=hs, nq=nq),
        grid=(batch, heads // hs, nq // 2),
        in_specs=[
            pl.BlockSpec((BLK, wide), lambda b, h, p: (b * nq + (nq - 1 - p), h)),
            pl.BlockSpec((BLK, wide), lambda b, h, p: (b * nq + p, h)),
            pl.BlockSpec((seq, wide), lambda b, h, p: (b, h)),
            pl.BlockSpec((seq, wide), lambda b, h, p: (b, h)),
        ],
        out_specs=pl.BlockSpec((None, None, 2, BLK, wide), lambda b, h, p: (b, p, 0, 0, h)),
        out_shape=jax.ShapeDtypeStruct((batch, nq // 2, 2, BLK, d), BF16),
        scratch_shapes=[
            pltpu.VMEM((2, BLK, wide), BF16),
            pltpu.VMEM((2, hs, BLK, HEAD_DIM), F32),
            pltpu.VMEM((2, hs, BLK, HEAD_DIM), F32),
        ],
        compiler_params=_params(("parallel", "parallel", "arbitrary")),
        name="stickbrk",
    )(qb, qb, kb, vb)


def _merge_ffn_body(x_ref, oa_ref, ob0_ref, ob1_ref, ga_ref, gb_ref, wa_ref, wb_ref, wo_ref,
                    gain_ref, win_ref, wout_ref, o_ref, *, d_ff):
    ob = jnp.concatenate([ob0_ref[...], ob1_ref[...]], axis=0)
    ya = _dot(oa_ref[...], wa_ref[...])
    yb = _dot(ob, wb_ref[...])
    merged = _sigmoid(ga_ref[...].astype(F32)) * ya + _sigmoid(gb_ref[...].astype(F32)) * yb
    x = x_ref[...] + _dot(merged.astype(BF16), wo_ref[...])
    xn = _rms_rows(x, gain_ref[...]).astype(BF16)
    h = _dot(xn, win_ref[...])
    gate = h[:, :d_ff]
    up = h[:, d_ff:]
    act = (gate * _sigmoid(gate) * up).astype(BF16)
    o_ref[...] = x + 0.5 * _dot(act, wout_ref[...])


def _merge_ffn(x, oa, ob, ga, gb, wa, wb, wo, gain, w_in, w_out, layer, *, nq):
    m, d = x.shape
    d_ff = w_out.shape[1]
    tile = pl.BlockSpec((2 * BLK, d), lambda i: (i, 0))

    def ob_block(which):
        def index(i):
            blk = lax.rem(2 * i + which, nq)
            return (lax.div(2 * i + which, nq), jnp.minimum(blk, nq - 1 - blk), jnp.where(blk < nq // 2, 1, 0), 0, 0)
        return pl.BlockSpec((None, None, None, BLK, d), index)

    return pl.pallas_call(
        functools.partial(_merge_ffn_body, d_ff=d_ff),
        grid=(m // (2 * BLK),),
        in_specs=[tile, tile, ob_block(0), ob_block(1), tile, tile,
                  _resident_layer(wa, layer), _resident_layer(wb, layer), _resident_layer(wo, layer),
                  _resident((1, d)), _resident_layer(w_in, layer), _resident_layer(w_out, layer)],
        out_specs=tile,
        out_shape=jax.ShapeDtypeStruct((m, d), F32),
        compiler_params=_params(("parallel",)),
        name="mergeffn",
    )(x, oa, ob, ob, ga, gb, wa, wb, wo, gain, w_in, w_out)


def _row_tile(m, want):
    tm = min(want, m)
    assert m % tm == 0, (m, tm)
    return tm


def kernel(x, ffn1_norm, ffn1_w_in, ffn1_w_out, mix_norm, w_in, dn_conv_w, dn_a_log, dn_dt_bias,
           dn_out_norm, sb_q_norm, sb_k_norm, w_branch_a, w_branch_b, w_out,
           ffn2_norm, ffn2_w_in, ffn2_w_out):
    batch, seq, d = x.shape
    depth = ffn1_norm.shape[0]
    heads = d // HEAD_DIM
    assert seq % (2 * BLK) == 0 and d % HEAD_DIM == 0 and 2 * heads <= HEAD_DIM
    m = batch * seq
    tm_ffn = _row_tile(m, 512)
    tm_proj = _row_tile(m, 512)
    hg = 8 if heads % 8 == 0 else 1
    hs = 4 if heads % 4 == 0 else 1

    n_ba = 2 * heads
    w_dn = w_in[:, :, :4 * d].astype(BF16)
    w_sb = w_in[:, :, 4 * d + n_ba:].astype(BF16)
    w_ba = jnp.pad(w_in[:, :, 4 * d:4 * d + n_ba], ((0, 0), (0, 0), (0, HEAD_DIM - n_ba))).astype(BF16)
    lane_pad = lambda v: jnp.pad(v, ((0, 0), (heads, HEAD_DIM - n_ba)))[:, None, :]
    avec = lane_pad(-jnp.exp(dn_a_log.astype(F32)))
    dtvec = lane_pad(dn_dt_bias.astype(F32))
    f1_in, f1_out = ffn1_w_in.astype(BF16), ffn1_w_out.astype(BF16)
    f2_in, f2_out = ffn2_w_in.astype(BF16), ffn2_w_out.astype(BF16)
    wa, wb, wo = w_branch_a.astype(BF16), w_branch_b.astype(BF16), w_out.astype(BF16)

    xf = x.reshape(m, d)
    for l in range(depth):
        xf = _ffn(xf, ffn1_norm[l][None], f1_in, f1_out, l, tm=tm_ffn)
        dnqkv, z, qb, kb, vb, ga, gb, ba = _proj(
            xf, mix_norm[l][None], w_dn, w_sb, w_ba, l, sb_q_norm[l][None], sb_k_norm[l][None],
            avec[l], dtvec[l], tm=tm_proj, heads=heads)
        oa = _deltanet(dnqkv, dn_conv_w, l, ba, z, dn_out_norm[l][None],
                       batch=batch, seq=seq, heads=heads, hg=hg)
        ob = _stick_breaking(qb, kb, vb, batch=batch, seq=seq, heads=heads, hs=hs)
        xf = _merge_ffn(xf, oa, ob, ga, gb, wa, wb, wo, ffn2_norm[l][None], f2_in, f2_out, l, nq=seq // BLK)
    return xf.reshape(batch, seq, d)
```
